```python
import jax, jax.numpy as jnp
from jax import lax
import numpy as np

D_MODEL = 2048
BATCH = 4
SEQ = 4096
DEPTH = 2

HEAD_DIM = 128
DIL_GROUPS = ((128, 1), (512, 4), (2048, 16))
HEADS_PER_GROUP = 4
A_HEADS = HEADS_PER_GROUP * len(DIL_GROUPS)
A_WIDTH = A_HEADS * HEAD_DIM
A_OUT = HEADS_PER_GROUP * HEAD_DIM
ALIBI_MAX_EXP = 8.0

GLA_HEADS = 4
GLA_DK = D_MODEL // 2
GLA_DV = D_MODEL
GLA_HK = GLA_DK // GLA_HEADS
GLA_HV = GLA_DV // GLA_HEADS
GLA_RANK = 16
GLA_TAU = 16.0
GLA_CHUNK = 64

N_BRANCH = 2
SPLIT_SIZES = (A_WIDTH, A_WIDTH, A_WIDTH, GLA_DK, GLA_DK, GLA_DV, GLA_DV, GLA_RANK, N_BRANCH * D_MODEL)
IN_COLS = sum(SPLIT_SIZES)

FFN_DENSE = 5632
N_EXPERTS = 8
TOP_K = 2
FFN_EXPERT = 7168

N_MOD = 6
LN_EPS = 1e-5
NORM_EPS = 1e-6
DN_ALPHA = (2 * DEPTH) ** 0.25
DN_BETA = (8 * DEPTH) ** -0.25

kernel_name = 'hybrid_dilated_gla_moe_deepnorm'


def layer_norm(x, g, b):
    xf = x.astype(jnp.float32)
    mu = xf.mean(-1, keepdims=True)
    var = jnp.square(xf - mu).mean(-1, keepdims=True)
    return ((xf - mu) * lax.rsqrt(var + LN_EPS) * g + b).astype(x.dtype)


def rms_norm(x, g):
    xf = x.astype(jnp.float32)
    return xf * lax.rsqrt(jnp.mean(jnp.square(xf), -1, keepdims=True) + NORM_EPS) * g


def alibi_slopes():
    return 2.0 ** (-ALIBI_MAX_EXP * jnp.arange(1, A_HEADS + 1, dtype=jnp.float32) / A_HEADS)


def dilated_window_group(q, k, v, slopes, window, dilation):
    B, S, H, Dh = q.shape
    band = window // dilation
    L = S // dilation
    nb = -(-L // band)
    Lp = nb * band

    def to_sub(t):
        t = t.reshape(B, L, dilation, H, Dh).transpose(0, 2, 3, 1, 4)
        t = jnp.pad(t, ((0, 0), (0, 0), (0, 0), (0, Lp - L), (0, 0)))
        return t.reshape(B, dilation, H, nb, band, Dh)

    def with_prev(t):
        prev = jnp.pad(t, ((0, 0), (0, 0), (0, 0), (1, 0), (0, 0), (0, 0)))[:, :, :, :-1]
        return jnp.concatenate([prev, t], axis=4)

    qs = to_sub(q)
    kb = with_prev(to_sub(k))
    vb = with_prev(to_sub(v))
    s = jnp.einsum('bdhnqe,bdhnke->bdhnqk', qs, kb, preferred_element_type=jnp.float32) * (Dh ** -0.5)
    qi = jnp.arange(band)[:, None] + band
    ki = jnp.arange(2 * band)[None, :]
    delta = qi - ki
    blk = jnp.arange(nb)[:, None, None]
    valid = (delta >= 0) & (delta <= band) & ((blk > 0) | (ki >= band))
    bias = -slopes[:, None, None, None] * (delta * dilation).astype(jnp.float32)
    s = jnp.where(valid, s + bias, -jnp.inf)
    m = s.max(-1, keepdims=True)
    p = jnp.exp(s - m)
    den = p.sum(-1, keepdims=True)
    o = jnp.einsum('bdhnqk,bdhnke->bdhnqe', p, vb.astype(jnp.float32)) / den
    lse = (m + jnp.log(den))[..., 0]

    def from_sub(t):
        t = t.reshape(B, dilation, H, Lp, *t.shape[5:])[:, :, :, :L]
        t = jnp.moveaxis(t, 3, 1)
        return t.reshape(B, S, H, *t.shape[4:])

    return from_sub(o), from_sub(lse)


def dilated_attention(aq, ak, av, slopes):
    B, S, _ = aq.shape
    q = aq.reshape(B, S, A_HEADS, HEAD_DIM)
    k = ak.reshape(B, S, A_HEADS, HEAD_DIM)
    v = av.reshape(B, S, A_HEADS, HEAD_DIM)
    outs, lses = [], []
    for g, (window, dil) in enumerate(DIL_GROUPS):
        hs = slice(g * HEADS_PER_GROUP, (g + 1) * HEADS_PER_GROUP)
        o, lse = dilated_window_group(q[:, :, hs], k[:, :, hs], v[:, :, hs], slopes[hs], window, dil)
        outs.append(o)
        lses.append(lse)
    w = jax.nn.softmax(jnp.stack(lses, 0), axis=0)[..., None]
    o = jnp.sum(jnp.stack(outs, 0) * w, axis=0)
    return o.reshape(B, S, A_OUT).astype(aq.dtype)


def gated_linear_attention(q, k, v, log_a):
    B, S, H, dk = q.shape
    dv = v.shape[-1]
    C = GLA_CHUNK
    nc = S // C

    def chunks(t):
        return t.reshape(B, nc, C, H, t.shape[-1]).transpose(0, 3, 1, 2, 4).astype(jnp.float32)

    q = chunks(q) * (dk ** -0.5)
    k = chunks(k)
    v = chunks(v)
    gc = jnp.cumsum(chunks(log_a), axis=3)
    g_ref = gc[:, :, :, C // 2 - 1:C // 2]
    att = jnp.einsum('bhnqe,bhnke->bhnqk', q * jnp.exp(gc - g_ref), k * jnp.exp(g_ref - gc))
    causal = jnp.tril(jnp.ones((C, C), dtype=bool))
    o_intra = jnp.einsum('bhnqk,bhnkf->bhnqf', jnp.where(causal, att, 0.0), v)
    g_tot = gc[:, :, :, -1:]
    q_in = q * jnp.exp(gc)
    k_out = k * jnp.exp(g_tot - gc)
    decay = jnp.exp(g_tot[:, :, :, 0])

    def step(state, xs):
        q_n, k_n, v_n, d_n = xs
        o_n = jnp.einsum('bhqe,bhef->bhqf', q_n, state)
        state = state * d_n[..., None] + jnp.einsum('bhke,bhkf->bhef', k_n, v_n)
        return state, o_n

    xs = (jnp.moveaxis(q_in, 2, 0), jnp.moveaxis(k_out, 2, 0), jnp.moveaxis(v, 2, 0), jnp.moveaxis(decay, 2, 0))
    _, o_inter = lax.scan(step, jnp.zeros((B, H, dk, dv), jnp.float32), xs)
    o = o_intra + jnp.moveaxis(o_inter, 0, 2)
    return o.transpose(0, 2, 3, 1, 4).reshape(B, S, H, dv)


def token_mix(h, w_in, w_alpha_up, b_alpha, gla_norm_g, b_merge, w_branch_a, w_branch_b, w_out, slopes):
    B, S, D = h.shape
    proj = jnp.einsum('bsd,de->bse', h, w_in)
    idx = []
    acc = 0
    for n in SPLIT_SIZES[:-1]:
        acc += n
        idx.append(acc)
    aq, ak, av, bq, bk, bv, br, b_low, g_logits = jnp.split(proj, idx, axis=-1)
    y_a = dilated_attention(aq, ak, av, slopes)
    log_a = jax.nn.log_sigmoid((b_low @ w_alpha_up + b_alpha).astype(jnp.float32)) / GLA_TAU
    o_b = gated_linear_attention(bq.reshape(B, S, GLA_HEADS, GLA_HK), bk.reshape(B, S, GLA_HEADS, GLA_HK),
                                 bv.reshape(B, S, GLA_HEADS, GLA_HV), log_a.reshape(B, S, GLA_HEADS, GLA_HK))
    o_b = rms_norm(o_b, gla_norm_g).reshape(B, S, GLA_DV)
    y_b = (jax.nn.silu(br.astype(jnp.float32)) * o_b).astype(h.dtype)
    gates = jax.nn.sigmoid(g_logits.reshape(B, S, N_BRANCH, D) + b_merge)
    mixed = gates[:, :, 0] * (y_a @ w_branch_a) + gates[:, :, 1] * (y_b @ w_branch_b)
    return mixed @ w_out


def swiglu(h, w1, w3, w2):
    return (jax.nn.silu(h @ w1) * (h @ w3)) @ w2


def moe_swiglu(h, w_router, w1, w3, w2):
    logits = (h @ w_router).astype(jnp.float32)
    top_v, top_i = lax.top_k(logits, TOP_K)
    wts = jax.nn.softmax(top_v, axis=-1)
    gate = jnp.sum(jax.nn.one_hot(top_i, N_EXPERTS, dtype=jnp.float32) * wts[..., None], axis=-2)
    out = jnp.zeros_like(h)
    for e in range(N_EXPERTS):
        out = out + gate[..., e:e + 1].astype(h.dtype) * swiglu(h, w1[e], w3[e], w2[e])
    return out


def setup_inputs(seed: int = 0) -> dict:
    key = jax.random.key(seed)
    ks = jax.random.split(key, 24)
    D = D_MODEL
    n_dense = (DEPTH + 1) // 2
    n_moe = DEPTH // 2
    nrm = lambda k, shape, s: jax.random.normal(k, shape, jnp.float32) * s
    col_scale = np.ones((IN_COLS,), np.float32)
    col_scale[2 * A_WIDTH:3 * A_WIDTH] = DN_BETA
    v_b0 = 3 * A_WIDTH + 2 * GLA_DK
    col_scale[v_b0:v_b0 + GLA_DV] = DN_BETA
    return {
        'x': nrm(ks[0], (BATCH, SEQ, D), 1.0),
        'c': nrm(ks[1], (BATCH, D), 1.0),
        'w_ada': nrm(ks[2], (DEPTH, D, N_MOD * D), 0.5 * D ** -0.5),
        'b_ada': nrm(ks[3], (DEPTH, N_MOD * D), 0.02),
        'ln_g': 1.0 + nrm(ks[4], (DEPTH, 2, D), 0.02),
        'ln_b': nrm(ks[5], (DEPTH, 2, D), 0.02),
        'w_in': nrm(ks[6], (DEPTH, D, IN_COLS), D ** -0.5) * jnp.asarray(col_scale),
        'w_alpha_up': nrm(ks[7], (DEPTH, GLA_RANK, GLA_DK), GLA_RANK ** -0.5),
        'b_alpha': nrm(ks[8], (DEPTH, GLA_DK), 0.1),
        'gla_norm_g': 1.0 + nrm(ks[9], (DEPTH, GLA_HV), 0.02),
        'b_merge': nrm(ks[10], (DEPTH, N_BRANCH, D), 0.1),
        'w_branch_a': nrm(ks[11], (DEPTH, A_OUT, D), A_OUT ** -0.5),
        'w_branch_b': nrm(ks[12], (DEPTH, GLA_DV, D), GLA_DV ** -0.5),
        'w_out': nrm(ks[13], (DEPTH, D, D), DN_BETA * D ** -0.5),
        'ffn_w1': nrm(ks[14], (n_dense, D, FFN_DENSE), D ** -0.5),
        'ffn_w3': nrm(ks[15], (n_dense, D, FFN_DENSE), D ** -0.5),
        'ffn_w2': nrm(ks[16], (n_dense, FFN_DENSE, D), DN_BETA * FFN_DENSE ** -0.5),
        'w_router': nrm(ks[17], (n_moe, D, N_EXPERTS), D ** -0.5),
        'moe_w1': nrm(ks[18], (n_moe, N_EXPERTS, D, FFN_EXPERT), D ** -0.5),
        'moe_w3': nrm(ks[19], (n_moe, N_EXPERTS, D, FFN_EXPERT), D ** -0.5),
        'moe_w2': nrm(ks[20], (n_moe, N_EXPERTS, FFN_EXPERT, D), DN_BETA * FFN_EXPERT ** -0.5),
    }


def reference(x, c, w_ada, b_ada, ln_g, ln_b, w_in, w_alpha_up, b_alpha, gla_norm_g, b_merge,
              w_branch_a, w_branch_b, w_out, ffn_w1, ffn_w3, ffn_w2, w_router, moe_w1, moe_w3, moe_w2):
    B = x.shape[0]
    slopes = alibi_slopes()
    cond = jax.nn.silu(c)
    for l in range(DEPTH):
        mod = (cond @ w_ada[l] + b_ada[l]).reshape(B, N_MOD, 1, D_MODEL)
        shift_m, scale_m, gate_m = mod[:, 0], mod[:, 1], mod[:, 2]
        shift_f, scale_f, gate_f = mod[:, 3], mod[:, 4], mod[:, 5]
        h = x * (1.0 + scale_m) + shift_m
        y = token_mix(h, w_in[l], w_alpha_up[l], b_alpha[l], gla_norm_g[l], b_merge[l],
                      w_branch_a[l], w_branch_b[l], w_out[l], slopes)
        x = layer_norm(DN_ALPHA * x + gate_m * y, ln_g[l, 0], ln_b[l, 0])
        h = x * (1.0 + scale_f) + shift_f
        if l % 2 == 0:
            y = swiglu(h, ffn_w1[l // 2], ffn_w3[l // 2], ffn_w2[l // 2])
        else:
            y = moe_swiglu(h, w_router[l // 2], moe_w1[l // 2], moe_w3[l // 2], moe_w2[l // 2])
        x = layer_norm(DN_ALPHA * x + gate_f * y, ln_g[l, 1], ln_b[l, 1])
    return x
```

```python
import functools

import jax
import jax.numpy as jnp
from jax import lax
from jax.experimental import pallas as pl
from jax.experimental.pallas import tpu as pltpu

F32 = jnp.float32
BF16 = jnp.bfloat16

D_MODEL = 2048
DEPTH = 2
HEAD_DIM = 128
DIL_GROUPS = ((128, 1), (512, 4), (2048, 16))
HEADS_PER_GROUP = 4
A_HEADS = HEADS_PER_GROUP * len(DIL_GROUPS)
A_WIDTH = A_HEADS * HEAD_DIM
A_OUT = HEADS_PER_GROUP * HEAD_DIM
ALIBI_MAX_EXP = 8.0
GLA_HEADS = 4
GLA_DK = D_MODEL // 2
GLA_DV = D_MODEL
GLA_HK = GLA_DK // GLA_HEADS
GLA_HV = GLA_DV // GLA_HEADS
GLA_RANK = 16
GLA_TAU = 16.0
GLA_CHUNK = 64
N_BRANCH = 2
FFN_DENSE = 5632
N_EXPERTS = 8
TOP_K = 2
FFN_EXPERT = 7168
N_MOD = 6
LN_EPS = 1e-5
NORM_EPS = 1e-6
DN_ALPHA = (2 * DEPTH) ** 0.25

LANE = 128
MXU_DIM = 256
VMEM_LIMIT_MB = 56

U_AQ, U_AK, U_AV = 0, 12, 24
U_BQ, U_BK, U_BV, U_BR = 36, 44, 52, 68
U_GA, U_GB = 84, 100
U_LOW = 116
N_UNITS = 120
N_PROJ = N_UNITS * LANE
BAND = 128

_REF_LOW0 = 3 * A_WIDTH + 2 * GLA_DK + 2 * GLA_DV
_REF_G0 = _REF_LOW0 + GLA_RANK


def _cparams(semantics, vmem_mb=VMEM_LIMIT_MB):
    return pltpu.CompilerParams(dimension_semantics=semantics, vmem_limit_bytes=vmem_mb << 20)


def _dot(a, b):
    return jnp.dot(a, b, preferred_element_type=F32)


def _dot_nt(a, b):
    return lax.dot_general(a, b, (((1,), (1,)), ((), ())), preferred_element_type=F32)


def _dot_tn(a, b):
    return lax.dot_general(a, b, (((0,), (0,)), ((), ())), preferred_element_type=F32)


def _sigmoid(x):
    return 1.0 / (1.0 + jnp.exp(-x))


def _silu(x):
    return x * _sigmoid(x)


def _layer_norm(z, g, b):
    mu = jnp.mean(z, axis=-1, keepdims=True)
    zc = z - mu
    var = jnp.mean(zc * zc, axis=-1, keepdims=True)
    return zc * lax.rsqrt(var + LN_EPS) * g + b


def _mod_kernel(c_ref, w_ref, b_ref, o_ref):
    s = _silu(c_ref[...]).astype(BF16)
    o_ref[0] = _dot(s, w_ref[0].astype(BF16)) + b_ref[0]


def _adaln_mod(c, w_ada, b_ada):
    B = c.shape[0]
    rows = 8
    tn = 1024
    c_pad = jnp.zeros((rows, D_MODEL), F32).at[:B].set(c)
    n_out = N_MOD * D_MODEL
    out = pl.pallas_call(
        _mod_kernel,
        grid=(DEPTH, n_out // tn),
        in_specs=[
            pl.BlockSpec((rows, D_MODEL), lambda l, j: (0, 0)),
            pl.BlockSpec((1, D_MODEL, tn), lambda l, j: (l, 0, j)),
            pl.BlockSpec((1, 1, tn), lambda l, j: (l, 0, j)),
        ],
        out_specs=pl.BlockSpec((1, rows, tn), lambda l, j: (l, 0, j)),
        out_shape=jax.ShapeDtypeStruct((DEPTH, rows, n_out), F32),
        compiler_params=_cparams(("arbitrary", "arbitrary")),
        name="adaln_mod",
    )(c_pad, w_ada, b_ada.reshape(DEPTH, 1, n_out))
    return out[:, :B].reshape(DEPTH, B, N_MOD, D_MODEL)


def _inproj_kernel(x_ref, mod_ref, w_ref, o_ref, h_ref):
    @pl.when(pl.program_id(1) == 0)
    def _():
        shift = mod_ref[0, 0:1, :]
        scale = mod_ref[0, 1:2, :]
        h_ref[...] = (x_ref[...] * (1.0 + scale) + shift).astype(BF16)

    o_ref[...] = _dot(h_ref[...], w_ref[...]).astype(o_ref.dtype)


def _in_projection(x2, mod_l, w_cat, seq):
    T = x2.shape[0]
    tm = min(1024, seq)
    tn = 1024
    per_b = seq // tm
    return pl.pallas_call(
        _inproj_kernel,
        grid=(T // tm, N_PROJ // tn),
        in_specs=[
            pl.BlockSpec((tm, D_MODEL), lambda i, j: (i, 0)),
            pl.BlockSpec((1, N_MOD, D_MODEL), lambda i, j: (i // per_b, 0, 0)),
            pl.BlockSpec((D_MODEL, tn), lambda i, j: (0, j)),
        ],
        out_specs=pl.BlockSpec((tm, tn), lambda i, j: (i, j)),
        out_shape=jax.ShapeDtypeStruct((T, N_PROJ), BF16),
        scratch_shapes=[pltpu.VMEM((tm, D_MODEL), BF16)],
        compiler_params=_cparams(("parallel", "arbitrary")),
        name="in_projection",
    )(x2, mod_l, w_cat)


def _attn_kernel(q_ref, kp_ref, kc_ref, vp_ref, vc_ref, o_ref, lse_ref, *, tq, bias_scale):
    n = pl.program_id(2)
    qi = lax.broadcasted_iota(jnp.int32, (BAND, BAND), 0)
    ki = lax.broadcasted_iota(jnp.int32, (BAND, BAND), 1)
    valid_prev = ki >= qi
    valid_cur = ki <= qi
    delta_prev = (qi + BAND - ki).astype(F32)
    delta_cur = (qi - ki).astype(F32)
    neg_inf = jnp.float32(-jnp.inf)
    sm_scale = HEAD_DIM ** -0.5
    for hh in range(HEADS_PER_GROUP):
        cs = slice(hh * HEAD_DIM, (hh + 1) * HEAD_DIM)
        slope = bias_scale[hh]
        for sb in range(tq // BAND):
            rs = slice(sb * BAND, (sb + 1) * BAND)
            q = q_ref[0, rs, cs]
            if sb == 0:
                k_prev, v_prev = kp_ref[0, :, cs], vp_ref[0, :, cs]
                vprev_mask = valid_prev & (n > 0)
            else:
                ps = slice((sb - 1) * BAND, sb * BAND)
                k_prev, v_prev = kc_ref[0, ps, cs], vc_ref[0, ps, cs]
                vprev_mask = valid_prev
            k_cur, v_cur = kc_ref[0, rs, cs], vc_ref[0, rs, cs]
            s_prev = _dot_nt(q, k_prev) * sm_scale - slope * delta_prev
            s_cur = _dot_nt(q, k_cur) * sm_scale - slope * delta_cur
            s_prev = jnp.where(vprev_mask, s_prev, neg_inf)
            s_cur = jnp.where(valid_cur, s_cur, neg_inf)
            m = jnp.maximum(jnp.max(s_prev, axis=1, keepdims=True), jnp.max(s_cur, axis=1, keepdims=True))
            p_prev = jnp.exp(s_prev - m)
            p_cur = jnp.exp(s_cur - m)
            den = jnp.sum(p_prev, axis=1, keepdims=True) + jnp.sum(p_cur, axis=1, keepdims=True)
            o = _dot(p_prev.astype(BF16), v_prev) + _dot(p_cur.astype(BF16), v_cur)
            o_ref[0, rs, cs] = o / den
            lse_ref[0, rs, cs] = jnp.broadcast_to(m + jnp.log(den), (BAND, HEAD_DIM))


def _dilated_group(proj, batch, seq, g):
    _, dil = DIL_GROUPS[g]
    L = seq // dil
    tq = min(512, L)
    blocks = tq // BAND
    w4 = HEADS_PER_GROUP * HEAD_DIM
    row_units = N_UNITS * LANE // w4
    proj3 = proj.reshape(batch, L, dil * N_PROJ)
    slopes = [2.0 ** (-ALIBI_MAX_EXP * (g * HEADS_PER_GROUP + h + 1) / A_HEADS) * dil
              for h in range(HEADS_PER_GROUP)]

    def col(unit):
        return lambda b, r, n: (b, n, r * row_units + (unit + HEADS_PER_GROUP * g) * LANE // w4)

    def col_prev(unit):
        return lambda b, r, n: (b, jnp.maximum(n * blocks - 1, 0),
                                r * row_units + (unit + HEADS_PER_GROUP * g) * LANE // w4)

    out_spec = pl.BlockSpec((1, tq, w4), lambda b, r, n: (b, n, r))
    o, lse = pl.pallas_call(
        functools.partial(_attn_kernel, tq=tq, bias_scale=slopes),
        grid=(batch, dil, L // tq),
        in_specs=[
            pl.BlockSpec((1, tq, w4), col(U_AQ)),
            pl.BlockSpec((1, BAND, w4), col_prev(U_AK)),
            pl.BlockSpec((1, tq, w4), col(U_AK)),
            pl.BlockSpec((1, BAND, w4), col_prev(U_AV)),
            pl.BlockSpec((1, tq, w4), col(U_AV)),
        ],
        out_specs=[out_spec, out_spec],
        out_shape=[jax.ShapeDtypeStruct((batch, L, dil * w4), F32)] * 2,
        compiler_params=_cparams(("parallel", "parallel", "arbitrary")),
        name=f"dilated_attn_g{g}",
    )(proj3, proj3, proj3, proj3, proj3)
    return o.reshape(batch * seq, w4), lse.reshape(batch * seq, w4)


GLA_GROUP = 256


def _gla_kernel(q_ref, k_ref, v_ref, r_ref, low_ref, whi_ref, wlo_ref, balpha_ref, gnorm_ref,
                y_ref, state_ref, qe_ref, ke_ref, oi_ref, *, tb):
    @pl.when(pl.program_id(2) == 0)
    def _():
        state_ref[...] = jnp.zeros_like(state_ref)

    C = GLA_CHUNK
    G = GLA_GROUP
    low = low_ref[...]
    z = _dot(low, whi_ref[...]) + _dot(low, wlo_ref[...]) + balpha_ref[...]
    log_a = (jnp.minimum(z, 0.0) - jnp.log(1.0 + jnp.exp(-jnp.abs(z)))) * (1.0 / GLA_TAU)

    ri = lax.broadcasted_iota(jnp.int32, (G, G), 0)
    ci = lax.broadcasted_iota(jnp.int32, (G, G), 1)
    same_chunk = (ri // C) == (ci // C)
    causal = same_chunk & (ci <= ri)
    tril = causal.astype(BF16)
    sel_r = lax.broadcasted_iota(jnp.int32, (G, (G // C) * LANE), 0)
    sel_c = lax.broadcasted_iota(jnp.int32, (G, (G // C) * LANE), 1)
    chunk_sel = ((sel_r // C) == (sel_c // LANE)).astype(BF16)

    for grp in range(tb // G):
        rows = slice(grp * G, (grp + 1) * G)
        la = log_a[rows]
        la_hi = la.astype(BF16)
        la_lo = (la - la_hi.astype(F32)).astype(BF16)
        gc = _dot(tril, la_hi) + _dot(tril, la_lo)
        gtot_cols = _dot_tn(la_hi, chunk_sel) + _dot_tn(la_lo, chunk_sel)
        q_g = q_ref[rows, :].astype(F32) * (GLA_HK ** -0.5)
        k_g = k_ref[rows, :].astype(F32)
        for c in range(G // C):
            cr = slice(c * C, (c + 1) * C)
            gcc = gc[cr]
            g_ref_pt = gcc[C // 2 - 1:C // 2]
            g_tot = gcc[C - 1:C]
            qc, kc = q_g[cr], k_g[cr]
            qe_ref[cr, :] = (qc * jnp.exp(gcc - g_ref_pt)).astype(BF16)
            ke_ref[cr, :] = (kc * jnp.exp(g_ref_pt - gcc)).astype(BF16)
            q_in = (qc * jnp.exp(gcc)).astype(BF16)
            k_out = (kc * jnp.exp(g_tot - gcc)).astype(BF16)
            state = state_ref[...]
            oi_ref[cr, :] = _dot(q_in, state.astype(BF16))
            decay = jnp.exp(gtot_cols[:, c * LANE:(c + 1) * LANE])
            upd = _dot_tn(k_out, v_ref[grp * G + c * C:grp * G + (c + 1) * C, :])
            state_ref[...] = jnp.concatenate(
                [state[:, j * LANE:(j + 1) * LANE] * decay for j in range(GLA_HV // LANE)], axis=1) + upd
        att = _dot_nt(qe_ref[...], ke_ref[...])
        att = jnp.where(causal, att, 0.0).astype(BF16)
        o = _dot(att, v_ref[rows, :]) + oi_ref[...]
        ms = jnp.mean(o * o, axis=-1, keepdims=True)
        o = o * lax.rsqrt(ms + NORM_EPS) * gnorm_ref[...]
        y_ref[rows, :] = (_silu(r_ref[rows, :].astype(F32)) * o).astype(y_ref.dtype)


def _gla_branch(proj, batch, seq, w_up_hi, w_up_lo, b_alpha, gnorm):
    T = batch * seq
    tb = 512
    per_b = seq // tb

    def rowcol(unit, width):
        return lambda b, h, n: (b * per_b + n, unit * LANE // width + h)

    return pl.pallas_call(
        functools.partial(_gla_kernel, tb=tb),
        grid=(batch, GLA_HEADS, per_b),
        in_specs=[
            pl.BlockSpec((tb, GLA_HK), rowcol(U_BQ, GLA_HK)),
            pl.BlockSpec((tb, GLA_HK), rowcol(U_BK, GLA_HK)),
            pl.BlockSpec((tb, GLA_HV), rowcol(U_BV, GLA_HV)),
            pl.BlockSpec((tb, GLA_HV), rowcol(U_BR, GLA_HV)),
            pl.BlockSpec((tb, LANE), lambda b, h, n: (b * per_b + n, U_LOW)),
            pl.BlockSpec((LANE, GLA_HK), lambda b, h, n: (0, h)),
            pl.BlockSpec((LANE, GLA_HK), lambda b, h, n: (0, h)),
            pl.BlockSpec((1, GLA_HK), lambda b, h, n: (0, h)),
            pl.BlockSpec((1, GLA_HV), lambda b, h, n: (0, 0)),
        ],
        out_specs=pl.BlockSpec((tb, GLA_HV), lambda b, h, n: (b * per_b + n, h)),
        out_shape=jax.ShapeDtypeStruct((T, GLA_DV), BF16),
        scratch_shapes=[
            pltpu.VMEM((GLA_HK, GLA_HV), F32),
            pltpu.VMEM((GLA_GROUP, GLA_HK), BF16),
            pltpu.VMEM((GLA_GROUP, GLA_HK), BF16),
            pltpu.VMEM((GLA_GROUP, GLA_HV), F32),
        ],
        compiler_params=_cparams(("parallel", "parallel", "arbitrary")),
        name="gla_branch",
    )(proj, proj, proj, proj, proj, w_up_hi, w_up_lo, b_alpha.reshape(1, GLA_DK), gnorm.reshape(1, GLA_HV))


def _branch_kernel(o0_ref, o1_ref, o2_ref, l0_ref, l1_ref, l2_ref, yb_ref, ga_ref, gb_ref, bm_ref,
                   wa_ref, wb_ref, out_ref, ya_ref):
    @pl.when(pl.program_id(1) == 0)
    def _():
        l0, l1, l2 = l0_ref[...], l1_ref[...], l2_ref[...]
        m = jnp.maximum(jnp.maximum(l0, l1), l2)
        e0, e1, e2 = jnp.exp(l0 - m), jnp.exp(l1 - m), jnp.exp(l2 - m)
        ya = (o0_ref[...] * e0 + o1_ref[...] * e1 + o2_ref[...] * e2) / (e0 + e1 + e2)
        ya_ref[...] = ya.astype(BF16)

    pa = _dot(ya_ref[...], wa_ref[...])
    pb = _dot(yb_ref[...], wb_ref[...])
    gate_a = _sigmoid(ga_ref[...].astype(F32) + bm_ref[0:1, :])
    gate_b = _sigmoid(gb_ref[...].astype(F32) + bm_ref[1:2, :])
    out_ref[...] = (gate_a * pa + gate_b * pb).astype(out_ref.dtype)


def _branch_mix(outs, lses, y_b, proj, b_merge, wa, wb):
    T = y_b.shape[0]
    tm, tn = 512, 512
    row = lambda i, j: (i, 0)
    return pl.pallas_call(
        _branch_kernel,
        grid=(T // tm, D_MODEL // tn),
        in_specs=[pl.BlockSpec((tm, A_OUT), row)] * 6 + [
            pl.BlockSpec((tm, GLA_DV), row),
            pl.BlockSpec((tm, tn), lambda i, j: (i, U_GA * LANE // tn + j)),
            pl.BlockSpec((tm, tn), lambda i, j: (i, U_GB * LANE // tn + j)),
            pl.BlockSpec((N_BRANCH, tn), lambda i, j: (0, j)),
            pl.BlockSpec((A_OUT, tn), lambda i, j: (0, j)),
            pl.BlockSpec((GLA_DV, tn), lambda i, j: (0, j)),
        ],
        out_specs=pl.BlockSpec((tm, tn), lambda i, j: (i, j)),
        out_shape=jax.ShapeDtypeStruct((T, D_MODEL), BF16),
        scratch_shapes=[pltpu.VMEM((tm, A_OUT), BF16)],
        compiler_params=_cparams(("parallel", "arbitrary")),
        name="branch_mix",
    )(*outs, *lses, y_b, proj, proj, b_merge, wa, wb)


def _outproj_kernel(mixed_ref, x_ref, mod_ref, w_ref, g_ref, b_ref, *out_refs, emit_h):
    y = _dot(mixed_ref[...], w_ref[...])
    gate = mod_ref[0, 2:3, :]
    xn = _layer_norm(DN_ALPHA * x_ref[...] + gate * y, g_ref[...], b_ref[...])
    out_refs[0][...] = xn
    if emit_h:
        out_refs[1][...] = xn * (1.0 + mod_ref[0, 4:5, :]) + mod_ref[0, 3:4, :]


def _out_projection(mixed, x2, mod_l, w_out, ln_g, ln_b, seq, emit_h):
    T = x2.shape[0]
    tm = 256
    per_b = seq // tm
    row = pl.BlockSpec((tm, D_MODEL), lambda i: (i, 0))
    vec = pl.BlockSpec((1, D_MODEL), lambda i: (0, 0))
    n_out = 2 if emit_h else 1
    outs = pl.pallas_call(
        functools.partial(_outproj_kernel, emit_h=emit_h),
        grid=(T // tm,),
        in_specs=[
            row, row,
            pl.BlockSpec((1, N_MOD, D_MODEL), lambda i: (i // per_b, 0, 0)),
            pl.BlockSpec((D_MODEL, D_MODEL), lambda i: (0, 0)),
            vec, vec,
        ],
        out_specs=[row] * n_out,
        out_shape=[jax.ShapeDtypeStruct((T, D_MODEL), F32)] * n_out,
        compiler_params=_cparams(("parallel",)),
        name="out_projection_ln",
    )(mixed, x2, mod_l, w_out, ln_g.reshape(1, D_MODEL), ln_b.reshape(1, D_MODEL))
    return outs if emit_h else (outs[0], None)


def _ffn_kernel(x_ref, mod_ref, w1_ref, w3_ref, w2_ref, g_ref, b_ref, o_ref, h_ref, acc_ref):
    f = pl.program_id(1)

    @pl.when(f == 0)
    def _():
        h_ref[...] = (x_ref[...] * (1.0 + mod_ref[0, 4:5, :]) + mod_ref[0, 3:4, :]).astype(BF16)
        acc_ref[...] = jnp.zeros_like(acc_ref)

    h = h_ref[...]
    act = (_silu(_dot(h, w1_ref[...])) * _dot(h, w3_ref[...])).astype(BF16)
    acc_ref[...] += _dot(act, w2_ref[...])

    @pl.when(f == pl.num_programs(1) - 1)
    def _():
        z = DN_ALPHA * x_ref[...] + mod_ref[0, 5:6, :] * acc_ref[...]
        o_ref[...] = _layer_norm(z, g_ref[...], b_ref[...])


def _dense_ffn(x2, mod_l, w1, w3, w2, ln_g, ln_b, seq):
    T = x2.shape[0]
    tm, tf = 512, 512
    per_b = seq // tm
    row = pl.BlockSpec((tm, D_MODEL), lambda i, f: (i, 0))
    vec = pl.BlockSpec((1, D_MODEL), lambda i, f: (0, 0))
    return pl.pallas_call(
        _ffn_kernel,
        grid=(T // tm, FFN_DENSE // tf),
        in_specs=[
            row,
            pl.BlockSpec((1, N_MOD, D_MODEL), lambda i, f: (i // per_b, 0, 0)),
            pl.BlockSpec((D_MODEL, tf), lambda i, f: (0, f)),
            pl.BlockSpec((D_MODEL, tf), lambda i, f: (0, f)),
            pl.BlockSpec((tf, D_MODEL), lambda i, f: (f, 0)),
            vec, vec,
        ],
        out_specs=row,
        out_shape=jax.ShapeDtypeStruct((T, D_MODEL), F32),
        scratch_shapes=[pltpu.VMEM((tm, D_MODEL), BF16), pltpu.VMEM((tm, D_MODEL), F32)],
        compiler_params=_cparams(("parallel", "arbitrary")),
        name="dense_ffn_ln",
    )(x2, mod_l, w1, w3, w2, ln_g.reshape(1, D_MODEL), ln_b.reshape(1, D_MODEL))


MOE_TM = 512
INFO_I, INFO_W, INFO_R = 0, 2, 4


def _router_kernel(h_ref, wr_ref, info_ref, cnt_ref, carry_ref, *, tm):
    @pl.when(pl.program_id(0) == 0)
    def _():
        carry_ref[...] = jnp.zeros_like(carry_ref)

    logits = _dot(h_ref[...].astype(BF16), wr_ref[...])
    lane = lax.broadcasted_iota(jnp.int32, (tm, LANE), 1).astype(F32)
    neg_inf = jnp.float32(-jnp.inf)
    lg = jnp.where(lane < N_EXPERTS, logits, neg_inf)
    m1 = jnp.max(lg, axis=1, keepdims=True)
    i1 = jnp.min(jnp.where(lg == m1, lane, float(LANE)), axis=1, keepdims=True)
    lg2 = jnp.where(lane == i1, neg_inf, lg)
    m2 = jnp.max(lg2, axis=1, keepdims=True)
    i2 = jnp.min(jnp.where(lg2 == m2, lane, float(LANE)), axis=1, keepdims=True)
    e = jnp.exp(m2 - m1)
    w1 = 1.0 / (1.0 + e)
    w2 = e / (1.0 + e)
    oh1 = lane == i1
    oh2 = lane == i2
    onehot = (oh1 | oh2).astype(BF16)
    ri = lax.broadcasted_iota(jnp.int32, (tm, tm), 0)
    ci = lax.broadcasted_iota(jnp.int32, (tm, tm), 1)
    strict_lower = (ci < ri).astype(BF16)
    rank = _dot(strict_lower, onehot) + carry_ref[...]
    r1 = jnp.sum(jnp.where(oh1, rank, 0.0), axis=1, keepdims=True)
    r2 = jnp.sum(jnp.where(oh2, rank, 0.0), axis=1, keepdims=True)
    carry = carry_ref[...] + jnp.sum(onehot.astype(F32), axis=0, keepdims=True)
    carry_ref[...] = carry
    cnt_ref[...] = jnp.broadcast_to(carry, cnt_ref.shape)
    info = jnp.where(lane == INFO_I, i1, 0.0)
    info = jnp.where(lane == INFO_I + 1, i2, info)
    info = jnp.where(lane == INFO_W, w1, info)
    info = jnp.where(lane == INFO_W + 1, w2, info)
    info = jnp.where(lane == INFO_R, r1, info)
    info = jnp.where(lane == INFO_R + 1, r2, info)
    info_ref[...] = info


def _router(h2, w_router_pad):
    T = h2.shape[0]
    tm = 512
    return pl.pallas_call(
        functools.partial(_router_kernel, tm=tm),
        grid=(T // tm,),
        in_specs=[
            pl.BlockSpec((tm, D_MODEL), lambda i: (i, 0)),
            pl.BlockSpec((D_MODEL, LANE), lambda i: (0, 0)),
        ],
        out_specs=[
            pl.BlockSpec((tm, LANE), lambda i: (i, 0)),
            pl.BlockSpec((8, LANE), lambda i: (0, 0)),
        ],
        out_shape=[jax.ShapeDtypeStruct((T, LANE), F32), jax.ShapeDtypeStruct((8, LANE), F32)],
        scratch_shapes=[pltpu.VMEM((1, LANE), F32)],
        compiler_params=_cparams(("arbitrary",)),
        name="moe_router",
    )(h2, w_router_pad)


def _row_copy(src_hbm, src_row, dst_vmem, dst_row, sem):
    return pltpu.make_async_copy(src_hbm.at[pl.ds(src_row, 1), :], dst_vmem.at[pl.ds(dst_row, 1), :], sem)


def _expert_kernel(tile_expert_ref, n_tiles_ref, slot_token_ref, h_hbm, w1_ref, w3_ref, w2_ref,
                   o_ref, rows_ref, hb_ref, acc_ref, sem, *, tm):
    i = pl.program_id(0)
    f = pl.program_id(1)
    live = i < n_tiles_ref[0]

    @pl.when(live & (f == 0))
    def _():
        base = i * tm

        def issue(r, carry):
            _row_copy(h_hbm, slot_token_ref[base + r], rows_ref, r, sem).start()
            return carry

        lax.fori_loop(0, tm, issue, 0)

        def wait(r, carry):
            _row_copy(h_hbm, 0, rows_ref, r, sem).wait()
            return carry

        lax.fori_loop(0, tm, wait, 0)
        hb_ref[...] = rows_ref[...].astype(BF16)
        acc_ref[...] = jnp.zeros_like(acc_ref)

    @pl.when(live)
    def _():
        h = hb_ref[...]
        act = (_silu(_dot(h, w1_ref[0])) * _dot(h, w3_ref[0])).astype(BF16)
        acc_ref[...] += _dot(act, w2_ref[0])

    @pl.when(live & (f == pl.num_programs(1) - 1))
    def _():
        o_ref[...] = acc_ref[...]

    @pl.when(jnp.logical_not(live) & (f == pl.num_programs(1) - 1))
    def _():
        o_ref[...] = jnp.zeros_like(o_ref)


def _expert_ffn(h2, tile_expert, n_tiles, slot_token, w1, w3, w2):
    tm, tf = MOE_TM, 512
    max_tiles = tile_expert.shape[0]
    nf = FFN_EXPERT // tf

    def live_tile(i, nt):
        return jnp.minimum(i, nt[0] - 1)

    def f_idx(i, f, nt):
        return jnp.where(i < nt[0], f, nf - 1)

    grid_spec = pltpu.PrefetchScalarGridSpec(
        num_scalar_prefetch=3,
        grid=(max_tiles, nf),
        in_specs=[
            pl.BlockSpec(memory_space=pl.ANY),
            pl.BlockSpec((1, D_MODEL, tf), lambda i, f, te, nt, st: (te[live_tile(i, nt)], 0, f_idx(i, f, nt))),
            pl.BlockSpec((1, D_MODEL, tf), lambda i, f, te, nt, st: (te[live_tile(i, nt)], 0, f_idx(i, f, nt))),
            pl.BlockSpec((1, tf, D_MODEL), lambda i, f, te, nt, st: (te[live_tile(i, nt)], f_idx(i, f, nt), 0)),
        ],
        out_specs=pl.BlockSpec((tm, D_MODEL), lambda i, f, te, nt, st: (i, 0)),
        scratch_shapes=[
            pltpu.VMEM((tm, D_MODEL), F32),
            pltpu.VMEM((tm, D_MODEL), BF16),
            pltpu.VMEM((tm, D_MODEL), F32),
            pltpu.SemaphoreType.DMA(()),
        ],
    )
    return pl.pallas_call(
        functools.partial(_expert_kernel, tm=tm),
        grid_spec=grid_spec,
        out_shape=jax.ShapeDtypeStruct((max_tiles * tm, D_MODEL), F32),
        compiler_params=_cparams(("arbitrary", "arbitrary")),
        name="moe_experts",
    )(tile_expert, n_tiles, slot_token, h2, w1, w3, w2)


def _combine_kernel(slot_ref, ys_hbm, x_ref, info_ref, mod_ref, g_ref, b_ref, o_ref, rows_ref, sem, *, tm):
    base = pl.program_id(0) * tm

    def issue(r, carry):
        for k in range(TOP_K):
            _row_copy(ys_hbm, slot_ref[TOP_K * (base + r) + k], rows_ref.at[k], r, sem).start()
        return carry

    lax.fori_loop(0, tm, issue, 0)

    def wait(r, carry):
        for k in range(TOP_K):
            _row_copy(ys_hbm, 0, rows_ref.at[k], r, sem).wait()
        return carry

    lax.fori_loop(0, tm, wait, 0)
    info = info_ref[...]
    y = info[:, INFO_W:INFO_W + 1] * rows_ref[0] + info[:, INFO_W + 1:INFO_W + 2] * rows_ref[1]
    z = DN_ALPHA * x_ref[...] + mod_ref[0, 5:6, :] * y
    o_ref[...] = _layer_norm(z, g_ref[...], b_ref[...])


def _moe_combine(slots, y_sorted, x2, info, mod_l, ln_g, ln_b, seq):
    T = x2.shape[0]
    tm = 256
    per_b = seq // tm
    grid_spec = pltpu.PrefetchScalarGridSpec(
        num_scalar_prefetch=1,
        grid=(T // tm,),
        in_specs=[
            pl.BlockSpec(memory_space=pl.ANY),
            pl.BlockSpec((tm, D_MODEL), lambda i, s: (i, 0)),
            pl.BlockSpec((tm, LANE), lambda i, s: (i, 0)),
            pl.BlockSpec((1, N_MOD, D_MODEL), lambda i, s: (i // per_b, 0, 0)),
            pl.BlockSpec((1, D_MODEL), lambda i, s: (0, 0)),
            pl.BlockSpec((1, D_MODEL), lambda i, s: (0, 0)),
        ],
        out_specs=pl.BlockSpec((tm, D_MODEL), lambda i, s: (i, 0)),
        scratch_shapes=[pltpu.VMEM((TOP_K, tm, D_MODEL), F32), pltpu.SemaphoreType.DMA(())],
    )
    return pl.pallas_call(
        functools.partial(_combine_kernel, tm=tm),
        grid_spec=grid_spec,
        out_shape=jax.ShapeDtypeStruct((T, D_MODEL), F32),
        compiler_params=_cparams(("arbitrary",)),
        name="moe_combine_ln",
    )(slots, y_sorted, x2, info, mod_l, ln_g.reshape(1, D_MODEL), ln_b.reshape(1, D_MODEL))


def _moe_sublayer(x2, h2, mod_l, w_router, w1, w3, w2, ln_g, ln_b, seq):
    T = x2.shape[0]
    tm = MOE_TM
    wr_pad = jnp.zeros((D_MODEL, LANE), BF16).at[:, :N_EXPERTS].set(w_router.astype(BF16))
    info, cnt = _router(h2, wr_pad)
    experts = info[:, INFO_I:INFO_I + TOP_K].astype(jnp.int32)
    ranks = info[:, INFO_R:INFO_R + TOP_K].astype(jnp.int32)
    counts = cnt[0, :N_EXPERTS].astype(jnp.int32)
    tiles_e = (counts + tm - 1) // tm
    tile_end = jnp.cumsum(tiles_e)
    tile_start = tile_end - tiles_e
    max_tiles = (T * TOP_K) // tm + N_EXPERTS
    slots = (tile_start * tm)[experts] + ranks
    tile_ids = jnp.arange(max_tiles, dtype=jnp.int32)
    tile_expert = jnp.minimum(jnp.sum(tile_ids[:, None] >= tile_end[None, :], axis=1), N_EXPERTS - 1).astype(jnp.int32)
    n_tiles = tile_end[-1:].astype(jnp.int32)
    token_ids = jnp.broadcast_to(jnp.arange(T, dtype=jnp.int32)[:, None], (T, TOP_K))
    slot_token = jnp.zeros((max_tiles * tm,), jnp.int32).at[slots.reshape(-1)].set(token_ids.reshape(-1))
    y_sorted = _expert_ffn(h2, tile_expert, n_tiles, slot_token, w1, w3, w2)
    return _moe_combine(slots.reshape(-1), y_sorted, x2, info, mod_l, ln_g, ln_b, seq)


def _pack_w_in(w_in_l):
    low = jnp.zeros((D_MODEL, N_PROJ - U_LOW * LANE), w_in_l.dtype).at[:, :GLA_RANK].set(w_in_l[:, _REF_LOW0:_REF_G0])
    return jnp.concatenate([w_in_l[:, :_REF_LOW0], w_in_l[:, _REF_G0:], low], axis=1).astype(BF16)


def _pack_w_up(w_up_l):
    pad = jnp.zeros((LANE, GLA_DK), F32).at[:GLA_RANK].set(w_up_l)
    hi = pad.astype(BF16)
    lo = (pad - hi.astype(F32)).astype(BF16)
    return hi, lo


def kernel(x, c, w_ada, b_ada, ln_g, ln_b, w_in, w_alpha_up, b_alpha, gla_norm_g, b_merge, w_branch_a,
           w_branch_b, w_out, ffn_w1, ffn_w3, ffn_w2, w_router, moe_w1, moe_w3, moe_w2):
    batch, seq, _ = x.shape
    T = batch * seq
    x2 = x.reshape(T, D_MODEL)
    mod = _adaln_mod(c, w_ada, b_ada)
    for l in range(DEPTH):
        mod_l = mod[l]
        proj = _in_projection(x2, mod_l, _pack_w_in(w_in[l]), seq)
        groups = [_dilated_group(proj, batch, seq, g) for g in range(len(DIL_GROUPS))]
        w_up_hi, w_up_lo = _pack_w_up(w_alpha_up[l])
        y_b = _gla_branch(proj, batch, seq, w_up_hi, w_up_lo, b_alpha[l], gla_norm_g[l])
        mixed = _branch_mix([g[0] for g in groups], [g[1] for g in groups], y_b, proj, b_merge[l],
                            w_branch_a[l].astype(BF16), w_branch_b[l].astype(BF16))
        moe_layer = l % 2 == 1
        x2, h2 = _out_projection(mixed, x2, mod_l, w_out[l].astype(BF16), ln_g[l, 0], ln_b[l, 0], seq, moe_layer)
        if moe_layer:
            e = l // 2
            x2 = _moe_sublayer(x2, h2, mod_l, w_router[e], moe_w1[e].astype(BF16), moe_w3[e].astype(BF16),
                               moe_w2[e].astype(BF16), ln_g[l, 1], ln_b[l, 1], seq)
        else:
            e = l // 2
            x2 = _dense_ffn(x2, mod_l, ffn_w1[e].astype(BF16), ffn_w3[e].astype(BF16), ffn_w2[e].astype(BF16),
                            ln_g[l, 1], ln_b[l, 1], seq)
    return x2.reshape(batch, seq, D_MODEL)
```

```python
import functools

import jax
import jax.numpy as jnp
from jax import lax
from jax.experimental import pallas as pl
from jax.experimental.pallas import tpu as pltpu

F32 = jnp.float32
BF16 = jnp.bfloat16

D_MODEL = 2048
DEPTH = 2
HEAD_DIM = 128
DIL_GROUPS = ((128, 1), (512, 4), (2048, 16))
HEADS_PER_GROUP = 4
A_HEADS = HEADS_PER_GROUP * len(DIL_GROUPS)
A_WIDTH = A_HEADS * HEAD_DIM
A_OUT = HEADS_PER_GROUP * HEAD_DIM
ALIBI_MAX_EXP = 8.0
GLA_HEADS = 4
GLA_DK = D_MODEL // 2
GLA_DV = D_MODEL
GLA_HK = GLA_DK // GLA_HEADS
GLA_HV = GLA_DV // GLA_HEADS
GLA_RANK = 16
GLA_TAU = 16.0
GLA_CHUNK = 64
N_BRANCH = 2
FFN_DENSE = 5632
N_EXPERTS = 8
TOP_K = 2
FFN_EXPERT = 7168
N_MOD = 6
LN_EPS = 1e-5
NORM_EPS = 1e-6
DN_ALPHA = (2 * DEPTH) ** 0.25

LANE = 128
MXU_DIM = 256
VMEM_LIMIT_MB = 56

U_GROUP = 12
U_A_END = 36
U_BQ, U_BK, U_BV, U_BR = 36, 44, 52, 68
U_GA, U_GB = 84, 100
U_LOW = 116
N_UNITS = 120
N_PROJ = N_UNITS * LANE
BAND = 128
W4 = HEADS_PER_GROUP * HEAD_DIM
PERM_TILE = 1024

_REF_LOW0 = 3 * A_WIDTH + 2 * GLA_DK + 2 * GLA_DV
_REF_G0 = _REF_LOW0 + GLA_RANK


def _cparams(semantics, vmem_mb=VMEM_LIMIT_MB):
    return pltpu.CompilerParams(dimension_semantics=semantics, vmem_limit_bytes=vmem_mb << 20)


def _dot(a, b):
    return jnp.dot(a, b, preferred_element_type=F32)


def _dot_nt(a, b):
    return lax.dot_general(a, b, (((1,), (1,)), ((), ())), preferred_element_type=F32)


def _dot_tn(a, b):
    return lax.dot_general(a, b, (((0,), (0,)), ((), ())), preferred_element_type=F32)


def _sigmoid(x):
    return 1.0 / (1.0 + jnp.exp(-x))


def _silu(x):
    return x * _sigmoid(x)


def _layer_norm(z, g, b):
    mu = jnp.mean(z, axis=-1, keepdims=True)
    zc = z - mu
    var = jnp.mean(zc * zc, axis=-1, keepdims=True)
    return zc * lax.rsqrt(var + LN_EPS) * g + b


def _mod_kernel(c_ref, w_ref, b_ref, o_ref):
    s = _silu(c_ref[...]).astype(BF16)
    o_ref[0] = _dot(s, w_ref[0].astype(BF16)) + b_ref[0]


def _adaln_mod(c, w_ada, b_ada):
    B = c.shape[0]
    rows = 8
    tn = 1024
    c_pad = jnp.zeros((rows, D_MODEL), F32).at[:B].set(c)
    n_out = N_MOD * D_MODEL
    out = pl.pallas_call(
        _mod_kernel,
        grid=(DEPTH, n_out // tn),
        in_specs=[
            pl.BlockSpec((rows, D_MODEL), lambda l, j: (0, 0)),
            pl.BlockSpec((1, D_MODEL, tn), lambda l, j: (l, 0, j)),
            pl.BlockSpec((1, 1, tn), lambda l, j: (l, 0, j)),
        ],
        out_specs=pl.BlockSpec((1, rows, tn), lambda l, j: (l, 0, j)),
        out_shape=jax.ShapeDtypeStruct((DEPTH, rows, n_out), F32),
        compiler_params=_cparams(("arbitrary", "arbitrary")),
        name="adaln_mod",
    )(c_pad, w_ada, b_ada.reshape(DEPTH, 1, n_out))
    return out[:, :B].reshape(DEPTH, B, N_MOD, D_MODEL)


def _unit_dilation(unit):
    return DIL_GROUPS[unit // U_GROUP][1] if unit < U_A_END else 1


def _inproj_kernel(x_ref, mod_ref, w_ref, o_ref, h_ref, slab_ref, *, tm, tn):
    j = pl.program_id(1)

    @pl.when(j == 0)
    def _():
        shift = mod_ref[0, 0:1, :]
        scale = mod_ref[0, 1:2, :]
        h_ref[...] = (x_ref[...] * (1.0 + scale) + shift).astype(BF16)

    acc = _dot(h_ref[...], w_ref[...])
    parts = tn // W4
    tile_dils = [[_unit_dilation((jj * tn + p * W4) // LANE) for p in range(parts)] for jj in range(N_PROJ // tn)]
    perm_tiles = [jj for jj, ds in enumerate(tile_dils) if any(d > 1 for d in ds)]
    is_perm = functools.reduce(jnp.logical_or, [j == jj for jj in perm_tiles])

    @pl.when(jnp.logical_not(is_perm))
    def _():
        o_ref[...] = acc.astype(o_ref.dtype)

    for jj in perm_tiles:
        @pl.when(j == jj)
        def _(jj=jj):
            for p, dil in enumerate(tile_dils[jj]):
                c0 = p * W4
                if dil == 1:
                    o_ref[:, c0:c0 + W4] = acc[:, c0:c0 + W4].astype(o_ref.dtype)
                    continue
                n = tm // dil
                for s in range(W4 // LANE):
                    slab_ref[s] = acc[:, c0 + s * LANE:c0 + (s + 1) * LANE]
                for s in range(W4 // LANE):
                    for r in range(dil):
                        o_ref[r * n:(r + 1) * n, c0 + s * LANE:c0 + (s + 1) * LANE] = (
                            slab_ref[s, pl.ds(r, n, stride=dil), :].astype(o_ref.dtype))


def _in_projection(x2, mod_l, w_cat, seq):
    T = x2.shape[0]
    tm = PERM_TILE
    tn = 1024
    per_b = seq // tm
    return pl.pallas_call(
        functools.partial(_inproj_kernel, tm=tm, tn=tn),
        grid=(T // tm, N_PROJ // tn),
        in_specs=[
            pl.BlockSpec((tm, D_MODEL), lambda i, j: (i, 0)),
            pl.BlockSpec((1, N_MOD, D_MODEL), lambda i, j: (i // per_b, 0, 0)),
            pl.BlockSpec((D_MODEL, tn), lambda i, j: (0, j)),
        ],
        out_specs=pl.BlockSpec((tm, tn), lambda i, j: (i, j)),
        out_shape=jax.ShapeDtypeStruct((T, N_PROJ), BF16),
        scratch_shapes=[pltpu.VMEM((tm, D_MODEL), BF16), pltpu.VMEM((W4 // LANE, tm, LANE), F32)],
        compiler_params=_cparams(("parallel", "arbitrary")),
        name="in_projection",
    )(x2, mod_l, w_cat)


LSE_LANES = LANE // HEADS_PER_GROUP


def _attn_kernel(*refs, c, nt, bias_scale, has_prev):
    if has_prev:
        q_ref, kp_ref, k_ref, vp_ref, v_ref, o_ref, lse_ref = refs
        first_run = pl.program_id(1) == 0
    else:
        q_ref, k_ref, v_ref, o_ref, lse_ref = refs
    qi = lax.broadcasted_iota(jnp.int32, (BAND, BAND), 0)
    ki = lax.broadcasted_iota(jnp.int32, (BAND, BAND), 1)
    valid_prev = ki >= qi
    valid_cur = ki <= qi
    delta_prev = (qi + BAND - ki).astype(F32)
    delta_cur = (qi - ki).astype(F32)
    neg_inf = jnp.float32(-jnp.inf)
    sm_scale = HEAD_DIM ** -0.5

    def rows(sb):
        if c >= BAND:
            per = c // BAND
            return sb // per, slice((sb % per) * BAND, (sb % per + 1) * BAND)
        per = BAND // c
        return slice(sb * per, (sb + 1) * per), slice(None)

    def load(ref, sb, cs):
        t, r = rows(sb)
        return ref[t, r, cs].reshape(BAND, cs.stop - cs.start)

    def store(ref, sb, cs, val):
        t, r = rows(sb)
        if c < BAND:
            val = val.reshape(BAND // c, c, cs.stop - cs.start)
        ref[t, r, cs] = val

    for hh in range(HEADS_PER_GROUP):
        cs = slice(hh * HEAD_DIM, (hh + 1) * HEAD_DIM)
        slope = bias_scale[hh]
        for sb in range(nt * c // BAND):
            q = load(q_ref, sb, cs)
            k_cur, v_cur = load(k_ref, sb, cs), load(v_ref, sb, cs)
            s_cur = jnp.where(valid_cur, _dot_nt(q, k_cur) * sm_scale - slope * delta_cur, neg_inf)
            m = jnp.max(s_cur, axis=1, keepdims=True)
            if sb > 0 or has_prev:
                if sb > 0:
                    k_prev, v_prev, mask = load(k_ref, sb - 1, cs), load(v_ref, sb - 1, cs), valid_prev
                else:
                    k_prev, v_prev = kp_ref[0, :, cs], vp_ref[0, :, cs]
                    mask = valid_prev & jnp.logical_not(first_run)
                s_prev = jnp.where(mask, _dot_nt(q, k_prev) * sm_scale - slope * delta_prev, neg_inf)
                m = jnp.maximum(m, jnp.max(s_prev, axis=1, keepdims=True))
                p_prev = jnp.exp(s_prev - m)
            p_cur = jnp.exp(s_cur - m)
            den = jnp.sum(p_cur, axis=1, keepdims=True)
            o = _dot(p_cur.astype(BF16), v_cur)
            if sb > 0 or has_prev:
                den = den + jnp.sum(p_prev, axis=1, keepdims=True)
                o = o + _dot(p_prev.astype(BF16), v_prev)
            store(o_ref, sb, cs, o / den)
            store(lse_ref, sb, slice(hh * LSE_LANES, (hh + 1) * LSE_LANES),
                  jnp.broadcast_to(m + jnp.log(den), (BAND, LSE_LANES)))


def _dilated_group(proj, batch, seq, g):
    _, dil = DIL_GROUPS[g]
    T = batch * seq
    tiles = T // PERM_TILE
    per_b = seq // PERM_TILE
    qcol, kcol, vcol = (g * U_GROUP * LANE // W4 + i for i in range(3))
    slopes = [2.0 ** (-ALIBI_MAX_EXP * (g * HEADS_PER_GROUP + h + 1) / A_HEADS) * dil
              for h in range(HEADS_PER_GROUP)]
    if dil == 1:
        c, nt = PERM_TILE, 1
        runs = proj.reshape(tiles, PERM_TILE, N_PROJ)
        bands = proj.reshape(T // BAND, BAND, N_PROJ)
        per_run = PERM_TILE // BAND
        cur = lambda col: pl.BlockSpec((1, c, W4), lambda b, n: (b * per_b + n, 0, col))
        prev = lambda col: pl.BlockSpec(
            (1, BAND, W4), lambda b, n: (jnp.maximum((b * per_b + n) * per_run - 1, 0), 0, col))
        grid = (batch, per_b)
        in_specs = [cur(qcol), prev(kcol), cur(kcol), prev(vcol), cur(vcol)]
        args = (runs, bands, runs, bands, runs)
        out_specs = [pl.BlockSpec((1, c, W4), lambda b, n: (b * per_b + n, 0, 0)),
                     pl.BlockSpec((1, c, LANE), lambda b, n: (b * per_b + n, 0, 0))]
        out_shape = [jax.ShapeDtypeStruct((tiles, c, W4), F32), jax.ShapeDtypeStruct((tiles, c, LANE), F32)]
        semantics = ("parallel", "arbitrary")
    else:
        c, nt = PERM_TILE // dil, per_b
        runs = proj.reshape(tiles, dil, c, N_PROJ)
        cur = lambda col: pl.BlockSpec((nt, None, c, W4), lambda b, r: (b, r, 0, col))
        grid = (batch, dil)
        in_specs = [cur(qcol), cur(kcol), cur(vcol)]
        args = (runs, runs, runs)
        out_specs = [pl.BlockSpec((nt, None, c, W4), lambda b, r: (b, r, 0, 0)),
                     pl.BlockSpec((nt, None, c, LANE), lambda b, r: (b, r, 0, 0))]
        out_shape = [jax.ShapeDtypeStruct((tiles, dil, c, W4), F32),
                     jax.ShapeDtypeStruct((tiles, dil, c, LANE), F32)]
        semantics = ("parallel", "parallel")
    o, lse = pl.pallas_call(
        functools.partial(_attn_kernel, c=c, nt=nt, bias_scale=slopes, has_prev=dil == 1),
        grid=grid,
        in_specs=in_specs,
        out_specs=out_specs,
        out_shape=out_shape,
        compiler_params=_cparams(semantics),
        name=f"dilated_attn_g{g}",
    )(*args)
    return o.reshape(T, W4), lse.reshape(T, LANE)


GLA_GROUP = 256


def _gla_kernel(q_ref, k_ref, v_ref, r_ref, low_ref, whi_ref, wlo_ref, balpha_ref, gnorm_ref,
                y_ref, state_ref, qe_ref, ke_ref, oi_ref, *, tb):
    @pl.when(pl.program_id(2) == 0)
    def _():
        state_ref[...] = jnp.zeros_like(state_ref)

    C = GLA_CHUNK
    G = GLA_GROUP
    low = low_ref[...]
    z = _dot(low, whi_ref[...]) + _dot(low, wlo_ref[...]) + balpha_ref[...]
    log_a = (jnp.minimum(z, 0.0) - jnp.log(1.0 + jnp.exp(-jnp.abs(z)))) * (1.0 / GLA_TAU)

    ri = lax.broadcasted_iota(jnp.int32, (G, G), 0)
    ci = lax.broadcasted_iota(jnp.int32, (G, G), 1)
    same_chunk = (ri // C) == (ci // C)
    causal = same_chunk & (ci <= ri)
    tril = causal.astype(BF16)
    sel_r = lax.broadcasted_iota(jnp.int32, (G, (G // C) * LANE), 0)
    sel_c = lax.broadcasted_iota(jnp.int32, (G, (G // C) * LANE), 1)
    chunk_sel = ((sel_r // C) == (sel_c // LANE)).astype(BF16)

    for grp in range(tb // G):
        rows = slice(grp * G, (grp + 1) * G)
        la = log_a[rows]
        la_hi = la.astype(BF16)
        la_lo = (la - la_hi.astype(F32)).astype(BF16)
        gc = _dot(tril, la_hi) + _dot(tril, la_lo)
        gtot_cols = _dot_tn(la_hi, chunk_sel) + _dot_tn(la_lo, chunk_sel)
        q_g = q_ref[rows, :].astype(F32) * (GLA_HK ** -0.5)
        k_g = k_ref[rows, :].astype(F32)
        for c in range(G // C):
            cr = slice(c * C, (c + 1) * C)
            gcc = gc[cr]
            g_ref_pt = gcc[C // 2 - 1:C // 2]
            g_tot = gcc[C - 1:C]
            qc, kc = q_g[cr], k_g[cr]
            qe_ref[cr, :] = (qc * jnp.exp(gcc - g_ref_pt)).astype(BF16)
            ke_ref[cr, :] = (kc * jnp.exp(g_ref_pt - gcc)).astype(BF16)
            q_in = (qc * jnp.exp(gcc)).astype(BF16)
            k_out = (kc * jnp.exp(g_tot - gcc)).astype(BF16)
            state = state_ref[...]
            oi_ref[cr, :] = _dot(q_in, state.astype(BF16))
            decay = jnp.exp(gtot_cols[:, c * LANE:(c + 1) * LANE])
            upd = _dot_tn(k_out, v_ref[grp * G + c * C:grp * G + (c + 1) * C, :])
            state_ref[...] = jnp.concatenate(
                [state[:, j * LANE:(j + 1) * LANE] * decay for j in range(GLA_HV // LANE)], axis=1) + upd
        att = _dot_nt(qe_ref[...], ke_ref[...])
        att = jnp.where(causal, att, 0.0).astype(BF16)
        o = _dot(att, v_ref[rows, :]) + oi_ref[...]
        ms = jnp.mean(o * o, axis=-1, keepdims=True)
        o = o * lax.rsqrt(ms + NORM_EPS) * gnorm_ref[...]
        y_ref[rows, :] = (_silu(r_ref[rows, :].astype(F32)) * o).astype(y_ref.dtype)


def _gla_branch(proj, batch, seq, w_up_hi, w_up_lo, b_alpha, gnorm):
    T = batch * seq
    tb = 512
    per_b = seq // tb

    def rowcol(unit, width):
        return lambda b, h, n: (b * per_b + n, unit * LANE // width + h)

    return pl.pallas_call(
        functools.partial(_gla_kernel, tb=tb),
        grid=(batch, GLA_HEADS, per_b),
        in_specs=[
            pl.BlockSpec((tb, GLA_HK), rowcol(U_BQ, GLA_HK)),
            pl.BlockSpec((tb, GLA_HK), rowcol(U_BK, GLA_HK)),
            pl.BlockSpec((tb, GLA_HV), rowcol(U_BV, GLA_HV)),
            pl.BlockSpec((tb, GLA_HV), rowcol(U_BR, GLA_HV)),
            pl.BlockSpec((tb, LANE), lambda b, h, n: (b * per_b + n, U_LOW)),
            pl.BlockSpec((LANE, GLA_HK), lambda b, h, n: (0, h)),
            pl.BlockSpec((LANE, GLA_HK), lambda b, h, n: (0, h)),
            pl.BlockSpec((1, GLA_HK), lambda b, h, n: (0, h)),
            pl.BlockSpec((1, GLA_HV), lambda b, h, n: (0, 0)),
        ],
        out_specs=pl.BlockSpec((tb, GLA_HV), lambda b, h, n: (b * per_b + n, h)),
        out_shape=jax.ShapeDtypeStruct((T, GLA_DV), BF16),
        scratch_shapes=[
            pltpu.VMEM((GLA_HK, GLA_HV), F32),
            pltpu.VMEM((GLA_GROUP, GLA_HK), BF16),
            pltpu.VMEM((GLA_GROUP, GLA_HK), BF16),
            pltpu.VMEM((GLA_GROUP, GLA_HV), F32),
        ],
        compiler_params=_cparams(("parallel", "parallel", "arbitrary")),
        name="gla_branch",
    )(proj, proj, proj, proj, proj, w_up_hi, w_up_lo, b_alpha.reshape(1, GLA_DK), gnorm.reshape(1, GLA_HV))


def _branch_kernel(o0_ref, o1_ref, o2_ref, l0_ref, l1_ref, l2_ref, yb_ref, ga_ref, gb_ref, bm_ref,
                   wa_ref, wb_ref, out_ref, ya_ref, oslab_ref, lslab_ref, *, tm):
    @pl.when(pl.program_id(1) == 0)
    def _():
        for gi, (o_ref, l_ref) in enumerate(((o1_ref, l1_ref), (o2_ref, l2_ref))):
            dil = DIL_GROUPS[gi + 1][1]
            n = tm // dil
            for r in range(dil):
                src = slice(r * n, (r + 1) * n)
                dst = pl.ds(r, n, stride=dil)
                for h in range(HEADS_PER_GROUP):
                    oslab_ref[gi, h, dst, :] = o_ref[src, h * HEAD_DIM:(h + 1) * HEAD_DIM]
                lslab_ref[gi, dst, :] = l_ref[src, :]
        l0, l1, l2 = l0_ref[...], lslab_ref[0], lslab_ref[1]
        m = jnp.maximum(jnp.maximum(l0, l1), l2)
        e0, e1, e2 = jnp.exp(l0 - m), jnp.exp(l1 - m), jnp.exp(l2 - m)
        inv = 1.0 / (e0 + e1 + e2)
        w0, w1, w2 = e0 * inv, e1 * inv, e2 * inv
        for h in range(HEADS_PER_GROUP):
            cs = slice(h * HEAD_DIM, (h + 1) * HEAD_DIM)
            lane = slice(h * LSE_LANES, h * LSE_LANES + 1)
            ya = o0_ref[:, cs] * w0[:, lane] + oslab_ref[0, h] * w1[:, lane] + oslab_ref[1, h] * w2[:, lane]
            ya_ref[:, cs] = ya.astype(BF16)

    pa = _dot(ya_ref[...], wa_ref[...])
    pb = _dot(yb_ref[...], wb_ref[...])
    gate_a = _sigmoid(ga_ref[...].astype(F32) + bm_ref[0:1, :])
    gate_b = _sigmoid(gb_ref[...].astype(F32) + bm_ref[1:2, :])
    out_ref[...] = (gate_a * pa + gate_b * pb).astype(out_ref.dtype)


def _branch_mix(outs, lses, y_b, proj, b_merge, wa, wb):
    T = y_b.shape[0]
    tm, tn = PERM_TILE, 512
    row = lambda i, j: (i, 0)
    n_dilated = len(DIL_GROUPS) - 1
    return pl.pallas_call(
        functools.partial(_branch_kernel, tm=tm),
        grid=(T // tm, D_MODEL // tn),
        in_specs=[pl.BlockSpec((tm, A_OUT), row)] * 3 + [pl.BlockSpec((tm, LANE), row)] * 3 + [
            pl.BlockSpec((tm, GLA_DV), row),
            pl.BlockSpec((tm, tn), lambda i, j: (i, U_GA * LANE // tn + j)),
            pl.BlockSpec((tm, tn), lambda i, j: (i, U_GB * LANE // tn + j)),
            pl.BlockSpec((N_BRANCH, tn), lambda i, j: (0, j)),
            pl.BlockSpec((A_OUT, tn), lambda i, j: (0, j)),
            pl.BlockSpec((GLA_DV, tn), lambda i, j: (0, j)),
        ],
        out_specs=pl.BlockSpec((tm, tn), lambda i, j: (i, j)),
        out_shape=jax.ShapeDtypeStruct((T, D_MODEL), BF16),
        scratch_shapes=[
            pltpu.VMEM((tm, A_OUT), BF16),
            pltpu.VMEM((n_dilated, HEADS_PER_GROUP, tm, HEAD_DIM), F32),
            pltpu.VMEM((n_dilated, tm, LANE), F32),
        ],
        compiler_params=_cparams(("parallel", "arbitrary")),
        name="branch_mix",
    )(*outs, *lses, y_b, proj, proj, b_merge, wa, wb)


def _outproj_kernel(mixed_ref, x_ref, mod_ref, w_ref, g_ref, b_ref, *out_refs, emit_h):
    y = _dot(mixed_ref[...], w_ref[...])
    gate = mod_ref[0, 2:3, :]
    xn = _layer_norm(DN_ALPHA * x_ref[...] + gate * y, g_ref[...], b_ref[...])
    out_refs[0][...] = xn
    if emit_h:
        out_refs[1][...] = xn * (1.0 + mod_ref[0, 4:5, :]) + mod_ref[0, 3:4, :]


def _out_projection(mixed, x2, mod_l, w_out, ln_g, ln_b, seq, emit_h):
    T = x2.shape[0]
    tm = 256
    per_b = seq // tm
    row = pl.BlockSpec((tm, D_MODEL), lambda i: (i, 0))
    vec = pl.BlockSpec((1, D_MODEL), lambda i: (0, 0))
    n_out = 2 if emit_h else 1
    outs = pl.pallas_call(
        functools.partial(_outproj_kernel, emit_h=emit_h),
        grid=(T // tm,),
        in_specs=[
            row, row,
            pl.BlockSpec((1, N_MOD, D_MODEL), lambda i: (i // per_b, 0, 0)),
            pl.BlockSpec((D_MODEL, D_MODEL), lambda i: (0, 0)),
            vec, vec,
        ],
        out_specs=[row] * n_out,
        out_shape=[jax.ShapeDtypeStruct((T, D_MODEL), F32)] * n_out,
        compiler_params=_cparams(("parallel",)),
        name="out_projection_ln",
    )(mixed, x2, mod_l, w_out, ln_g.reshape(1, D_MODEL), ln_b.reshape(1, D_MODEL))
    return outs if emit_h else (outs[0], None)


def _ffn_kernel(x_ref, mod_ref, w1_ref, w3_ref, w2_ref, g_ref, b_ref, o_ref, h_ref, acc_ref):
    f = pl.program_id(1)

    @pl.when(f == 0)
    def _():
        h_ref[...] = (x_ref[...] * (1.0 + mod_ref[0, 4:5, :]) + mod_ref[0, 3:4, :]).astype(BF16)
        acc_ref[...] = jnp.zeros_like(acc_ref)

    h = h_ref[...]
    act = (_silu(_dot(h, w1_ref[...])) * _dot(h, w3_ref[...])).astype(BF16)
    acc_ref[...] += _dot(act, w2_ref[...])

    @pl.when(f == pl.num_programs(1) - 1)
    def _():
        z = DN_ALPHA * x_ref[...] + mod_ref[0, 5:6, :] * acc_ref[...]
        o_ref[...] = _layer_norm(z, g_ref[...], b_ref[...])


def _dense_ffn(x2, mod_l, w1, w3, w2, ln_g, ln_b, seq):
    T = x2.shape[0]
    tm, tf = 512, 512
    per_b = seq // tm
    row = pl.BlockSpec((tm, D_MODEL), lambda i, f: (i, 0))
    vec = pl.BlockSpec((1, D_MODEL), lambda i, f: (0, 0))
    return pl.pallas_call(
        _ffn_kernel,
        grid=(T // tm, FFN_DENSE // tf),
        in_specs=[
            row,
            pl.BlockSpec((1, N_MOD, D_MODEL), lambda i, f: (i // per_b, 0, 0)),
            pl.BlockSpec((D_MODEL, tf), lambda i, f: (0, f)),
            pl.BlockSpec((D_MODEL, tf), lambda i, f: (0, f)),
            pl.BlockSpec((tf, D_MODEL), lambda i, f: (f, 0)),
            vec, vec,
        ],
        out_specs=row,
        out_shape=jax.ShapeDtypeStruct((T, D_MODEL), F32),
        scratch_shapes=[pltpu.VMEM((tm, D_MODEL), BF16), pltpu.VMEM((tm, D_MODEL), F32)],
        compiler_params=_cparams(("parallel", "arbitrary")),
        name="dense_ffn_ln",
    )(x2, mod_l, w1, w3, w2, ln_g.reshape(1, D_MODEL), ln_b.reshape(1, D_MODEL))


MOE_TM = 512
INFO_I, INFO_W, INFO_R = 0, 2, 4


def _router_kernel(h_ref, wr_ref, info_ref, cnt_ref, carry_ref, *, tm):
    @pl.when(pl.program_id(0) == 0)
    def _():
        carry_ref[...] = jnp.zeros_like(carry_ref)

    logits = _dot(h_ref[...].astype(BF16), wr_ref[...])
    lane = lax.broadcasted_iota(jnp.int32, (tm, LANE), 1).astype(F32)
    neg_inf = jnp.float32(-jnp.inf)
    lg = jnp.where(lane < N_EXPERTS, logits, neg_inf)
    m1 = jnp.max(lg, axis=1, keepdims=True)
    i1 = jnp.min(jnp.where(lg == m1, lane, float(LANE)), axis=1, keepdims=True)
    lg2 = jnp.where(lane == i1, neg_inf, lg)
    m2 = jnp.max(lg2, axis=1, keepdims=True)
    i2 = jnp.min(jnp.where(lg2 == m2, lane, float(LANE)), axis=1, keepdims=True)
    e = jnp.exp(m2 - m1)
    w1 = 1.0 / (1.0 + e)
    w2 = e / (1.0 + e)
    oh1 = lane == i1
    oh2 = lane == i2
    onehot = (oh1 | oh2).astype(BF16)
    ri = lax.broadcasted_iota(jnp.int32, (tm, tm), 0)
    ci = lax.broadcasted_iota(jnp.int32, (tm, tm), 1)
    strict_lower = (ci < ri).astype(BF16)
    rank = _dot(strict_lower, onehot) + carry_ref[...]
    r1 = jnp.sum(jnp.where(oh1, rank, 0.0), axis=1, keepdims=True)
    r2 = jnp.sum(jnp.where(oh2, rank, 0.0), axis=1, keepdims=True)
    carry = carry_ref[...] + jnp.sum(onehot.astype(F32), axis=0, keepdims=True)
    carry_ref[...] = carry
    cnt_ref[...] = jnp.broadcast_to(carry, cnt_ref.shape)
    info = jnp.where(lane == INFO_I, i1, 0.0)
    info = jnp.where(lane == INFO_I + 1, i2, info)
    info = jnp.where(lane == INFO_W, w1, info)
    info = jnp.where(lane == INFO_W + 1, w2, info)
    info = jnp.where(lane == INFO_R, r1, info)
    info = jnp.where(lane == INFO_R + 1, r2, info)
    info_ref[...] = info


def _router(h2, w_router_pad):
    T = h2.shape[0]
    tm = 512
    return pl.pallas_call(
        functools.partial(_router_kernel, tm=tm),
        grid=(T // tm,),
        in_specs=[
            pl.BlockSpec((tm, D_MODEL), lambda i: (i, 0)),
            pl.BlockSpec((D_MODEL, LANE), lambda i: (0, 0)),
        ],
        out_specs=[
            pl.BlockSpec((tm, LANE), lambda i: (i, 0)),
            pl.BlockSpec((8, LANE), lambda i: (0, 0)),
        ],
        out_shape=[jax.ShapeDtypeStruct((T, LANE), F32), jax.ShapeDtypeStruct((8, LANE), F32)],
        scratch_shapes=[pltpu.VMEM((1, LANE), F32)],
        compiler_params=_cparams(("arbitrary",)),
        name="moe_router",
    )(h2, w_router_pad)


def _row_copy(src_hbm, src_row, dst_vmem, dst_row, sem):
    return pltpu.make_async_copy(src_hbm.at[pl.ds(src_row, 1), :], dst_vmem.at[pl.ds(dst_row, 1), :], sem)


def _expert_kernel(tile_expert_ref, n_tiles_ref, slot_token_ref, h_hbm, w1_ref, w3_ref, w2_ref,
                   o_ref, rows_ref, hb_ref, acc_ref, sem, *, tm):
    i = pl.program_id(0)
    f = pl.program_id(1)
    live = i < n_tiles_ref[0]

    @pl.when(live & (f == 0))
    def _():
        base = i * tm

        def issue(r, carry):
            _row_copy(h_hbm, slot_token_ref[base + r], rows_ref, r, sem).start()
            return carry

        lax.fori_loop(0, tm, issue, 0)

        def wait(r, carry):
            _row_copy(h_hbm, 0, rows_ref, r, sem).wait()
            return carry

        lax.fori_loop(0, tm, wait, 0)
        hb_ref[...] = rows_ref[...].astype(BF16)
        acc_ref[...] = jnp.zeros_like(acc_ref)

    @pl.when(live)
    def _():
        h = hb_ref[...]
        act = (_silu(_dot(h, w1_ref[0])) * _dot(h, w3_ref[0])).astype(BF16)
        acc_ref[...] += _dot(act, w2_ref[0])

    @pl.when(live & (f == pl.num_programs(1) - 1))
    def _():
        o_ref[...] = acc_ref[...]

    @pl.when(jnp.logical_not(live) & (f == pl.num_programs(1) - 1))
    def _():
        o_ref[...] = jnp.zeros_like(o_ref)


def _expert_ffn(h2, tile_expert, n_tiles, slot_token, w1, w3, w2):
    tm, tf = MOE_TM, 512
    max_tiles = tile_expert.shape[0]
    nf = FFN_EXPERT // tf

    def live_tile(i, nt):
        return jnp.minimum(i, nt[0] - 1)

    def f_idx(i, f, nt):
        return jnp.where(i < nt[0], f, nf - 1)

    grid_spec = pltpu.PrefetchScalarGridSpec(
        num_scalar_prefetch=3,
        grid=(max_tiles, nf),
        in_specs=[
            pl.BlockSpec(memory_space=pl.ANY),
            pl.BlockSpec((1, D_MODEL, tf), lambda i, f, te, nt, st: (te[live_tile(i, nt)], 0, f_idx(i, f, nt))),
            pl.BlockSpec((1, D_MODEL, tf), lambda i, f, te, nt, st: (te[live_tile(i, nt)], 0, f_idx(i, f, nt))),
            pl.BlockSpec((1, tf, D_MODEL), lambda i, f, te, nt, st: (te[live_tile(i, nt)], f_idx(i, f, nt), 0)),
        ],
        out_specs=pl.BlockSpec((tm, D_MODEL), lambda i, f, te, nt, st: (i, 0)),
        scratch_shapes=[
            pltpu.VMEM((tm, D_MODEL), F32),
            pltpu.VMEM((tm, D_MODEL), BF16),
            pltpu.VMEM((tm, D_MODEL), F32),
            pltpu.SemaphoreType.DMA(()),
        ],
    )
    return pl.pallas_call(
        functools.partial(_expert_kernel, tm=tm),
        grid_spec=grid_spec,
        out_shape=jax.ShapeDtypeStruct((max_tiles * tm, D_MODEL), F32),
        compiler_params=_cparams(("arbitrary", "arbitrary")),
        name="moe_experts",
    )(tile_expert, n_tiles, slot_token, h2, w1, w3, w2)


def _combine_kernel(slot_ref, ys_hbm, x_ref, info_ref, mod_ref, g_ref, b_ref, o_ref, rows_ref, sem, *, tm):
    base = pl.program_id(0) * tm

    def issue(r, carry):
        for k in range(TOP_K):
            _row_copy(ys_hbm, slot_ref[TOP_K * (base + r) + k], rows_ref.at[k], r, sem).start()
        return carry

    lax.fori_loop(0, tm, issue, 0)

    def wait(r, carry):
        for k in range(TOP_K):
            _row_copy(ys_hbm, 0, rows_ref.at[k], r, sem).wait()
        return carry

    lax.fori_loop(0, tm, wait, 0)
    info = info_ref[...]
    y = info[:, INFO_W:INFO_W + 1] * rows_ref[0] + info[:, INFO_W + 1:INFO_W + 2] * rows_ref[1]
    z = DN_ALPHA * x_ref[...] + mod_ref[0, 5:6, :] * y
    o_ref[...] = _layer_norm(z, g_ref[...], b_ref[...])


def _moe_combine(slots, y_sorted, x2, info, mod_l, ln_g, ln_b, seq):
    T = x2.shape[0]
    tm = 256
    per_b = seq // tm
    grid_spec = pltpu.PrefetchScalarGridSpec(
        num_scalar_prefetch=1,
        grid=(T // tm,),
        in_specs=[
            pl.BlockSpec(memory_space=pl.ANY),
            pl.BlockSpec((tm, D_MODEL), lambda i, s: (i, 0)),
            pl.BlockSpec((tm, LANE), lambda i, s: (i, 0)),
            pl.BlockSpec((1, N_MOD, D_MODEL), lambda i, s: (i // per_b, 0, 0)),
            pl.BlockSpec((1, D_MODEL), lambda i, s: (0, 0)),
            pl.BlockSpec((1, D_MODEL), lambda i, s: (0, 0)),
        ],
        out_specs=pl.BlockSpec((tm, D_MODEL), lambda i, s: (i, 0)),
        scratch_shapes=[pltpu.VMEM((TOP_K, tm, D_MODEL), F32), pltpu.SemaphoreType.DMA(())],
    )
    return pl.pallas_call(
        functools.partial(_combine_kernel, tm=tm),
        grid_spec=grid_spec,
        out_shape=jax.ShapeDtypeStruct((T, D_MODEL), F32),
        compiler_params=_cparams(("arbitrary",)),
        name="moe_combine_ln",
    )(slots, y_sorted, x2, info, mod_l, ln_g.reshape(1, D_MODEL), ln_b.reshape(1, D_MODEL))


def _moe_sublayer(x2, h2, mod_l, w_router, w1, w3, w2, ln_g, ln_b, seq):
    T = x2.shape[0]
    tm = MOE_TM
    wr_pad = jnp.zeros((D_MODEL, LANE), BF16).at[:, :N_EXPERTS].set(w_router.astype(BF16))
    info, cnt = _router(h2, wr_pad)
    experts = info[:, INFO_I:INFO_I + TOP_K].astype(jnp.int32)
    ranks = info[:, INFO_R:INFO_R + TOP_K].astype(jnp.int32)
    counts = cnt[0, :N_EXPERTS].astype(jnp.int32)
    tiles_e = (counts + tm - 1) // tm
    tile_end = jnp.cumsum(tiles_e)
    tile_start = tile_end - tiles_e
    max_tiles = (T * TOP_K) // tm + N_EXPERTS
    slots = (tile_start * tm)[experts] + ranks
    tile_ids = jnp.arange(max_tiles, dtype=jnp.int32)
    tile_expert = jnp.minimum(jnp.sum(tile_ids[:, None] >= tile_end[None, :], axis=1), N_EXPERTS - 1).astype(jnp.int32)
    n_tiles = tile_end[-1:].astype(jnp.int32)
    token_ids = jnp.broadcast_to(jnp.arange(T, dtype=jnp.int32)[:, None], (T, TOP_K))
    slot_token = jnp.zeros((max_tiles * tm,), jnp.int32).at[slots.reshape(-1)].set(token_ids.reshape(-1))
    y_sorted = _expert_ffn(h2, tile_expert, n_tiles, slot_token, w1, w3, w2)
    return _moe_combine(slots.reshape(-1), y_sorted, x2, info, mod_l, ln_g, ln_b, seq)


def _pack_w_in(w_in_l):
    low = jnp.zeros((D_MODEL, N_PROJ - U_LOW * LANE), w_in_l.dtype).at[:, :GLA_RANK].set(w_in_l[:, _REF_LOW0:_REF_G0])
    a_cols = [w_in_l[:, part * A_WIDTH + g * W4:part * A_WIDTH + (g + 1) * W4]
              for g in range(len(DIL_GROUPS)) for part in range(3)]
    return jnp.concatenate(a_cols + [w_in_l[:, 3 * A_WIDTH:_REF_LOW0], w_in_l[:, _REF_G0:], low], axis=1).astype(BF16)


def _pack_w_up(w_up_l):
    pad = jnp.zeros((LANE, GLA_DK), F32).at[:GLA_RANK].set(w_up_l)
    hi = pad.astype(BF16)
    lo = (pad - hi.astype(F32)).astype(BF16)
    return hi, lo


def kernel(x, c, w_ada, b_ada, ln_g, ln_b, w_in, w_alpha_up, b_alpha, gla_norm_g, b_merge, w_branch_a,
           w_branch_b, w_out, ffn_w1, ffn_w3, ffn_w2, w_router, moe_w1, moe_w3, moe_w2):
    batch, seq, _ = x.shape
    T = batch * seq
    x2 = x.reshape(T, D_MODEL)
    mod = _adaln_mod(c, w_ada, b_ada)
    for l in range(DEPTH):
        mod_l = mod[l]
        proj = _in_projection(x2, mod_l, _pack_w_in(w_in[l]), seq)
        groups = [_dilated_group(proj, batch, seq, g) for g in range(len(DIL_GROUPS))]
        w_up_hi, w_up_lo = _pack_w_up(w_alpha_up[l])
        y_b = _gla_branch(proj, batch, seq, w_up_hi, w_up_lo, b_alpha[l], gla_norm_g[l])
        mixed = _branch_mix([g[0] for g in groups], [g[1] for g in groups], y_b, proj, b_merge[l],
                            w_branch_a[l].astype(BF16), w_branch_b[l].astype(BF16))
        moe_layer = l % 2 == 1
        x2, h2 = _out_projection(mixed, x2, mod_l, w_out[l].astype(BF16), ln_g[l, 0], ln_b[l, 0], seq, moe_layer)
        if moe_layer:
            e = l // 2
            x2 = _moe_sublayer(x2, h2, mod_l, w_router[e], moe_w1[e].astype(BF16), moe_w3[e].astype(BF16),
                               moe_w2[e].astype(BF16), ln_g[l, 1], ln_b[l, 1], seq)
        else:
            e = l // 2
            x2 = _dense_ffn(x2, mod_l, ffn_w1[e].astype(BF16), ffn_w3[e].astype(BF16), ffn_w2[e].astype(BF16),
                            ln_g[l, 1], ln_b[l, 1], seq)
    return x2.reshape(batch, seq, D_MODEL)
```

```python
import functools

import jax
import jax.numpy as jnp
from jax import lax
from jax.experimental import pallas as pl
from jax.experimental.pallas import tpu as pltpu

F32 = jnp.float32
BF16 = jnp.bfloat16

D_MODEL = 2048
DEPTH = 2
HEAD_DIM = 128
DIL_GROUPS = ((128, 1), (512, 4), (2048, 16))
HEADS_PER_GROUP = 4
A_HEADS = HEADS_PER_GROUP * len(DIL_GROUPS)
A_WIDTH = A_HEADS * HEAD_DIM
A_OUT = HEADS_PER_GROUP * HEAD_DIM
ALIBI_MAX_EXP = 8.0
GLA_HEADS = 4
GLA_DK = D_MODEL // 2
GLA_DV = D_MODEL
GLA_HK = GLA_DK // GLA_HEADS
GLA_HV = GLA_DV // GLA_HEADS
GLA_RANK = 16
GLA_TAU = 16.0
GLA_CHUNK = 64
N_BRANCH = 2
FFN_DENSE = 5632
N_EXPERTS = 8
TOP_K = 2
FFN_EXPERT = 7168
N_MOD = 6
LN_EPS = 1e-5
NORM_EPS = 1e-6
DN_ALPHA = (2 * DEPTH) ** 0.25

LANE = 128
MXU_DIM = 256
VMEM_LIMIT_MB = 56

U_GROUP = 12
U_A_END = 36
U_BQ, U_BK, U_BV, U_BR = 36, 44, 52, 68
U_GA, U_GB = 84, 100
U_LOW = 116
N_UNITS = 120
N_PROJ = N_UNITS * LANE
BAND = 128
W4 = HEADS_PER_GROUP * HEAD_DIM
PERM_TILE = 1024

_REF_LOW0 = 3 * A_WIDTH + 2 * GLA_DK + 2 * GLA_DV
_REF_G0 = _REF_LOW0 + GLA_RANK


def _cparams(semantics, vmem_mb=VMEM_LIMIT_MB):
    return pltpu.CompilerParams(dimension_semantics=semantics, vmem_limit_bytes=vmem_mb << 20)


def _dot(a, b):
    return jnp.dot(a, b, preferred_element_type=F32)


def _dot_nt(a, b):
    return lax.dot_general(a, b, (((1,), (1,)), ((), ())), preferred_element_type=F32)


def _dot_tn(a, b):
    return lax.dot_general(a, b, (((0,), (0,)), ((), ())), preferred_element_type=F32)


def _sigmoid(x):
    return 1.0 / (1.0 + jnp.exp(-x))


def _silu(x):
    return x * _sigmoid(x)


def _layer_norm(z, g, b):
    mu = jnp.mean(z, axis=-1, keepdims=True)
    zc = z - mu
    var = jnp.mean(zc * zc, axis=-1, keepdims=True)
    return zc * lax.rsqrt(var + LN_EPS) * g + b


def _mod_kernel(c_ref, w_ref, b_ref, o_ref):
    s = _silu(c_ref[...]).astype(BF16)
    o_ref[0] = _dot(s, w_ref[0].astype(BF16)) + b_ref[0]


def _adaln_mod(c, w_ada, b_ada):
    B = c.shape[0]
    rows = 8
    tn = 1024
    c_pad = jnp.zeros((rows, D_MODEL), F32).at[:B].set(c)
    n_out = N_MOD * D_MODEL
    out = pl.pallas_call(
        _mod_kernel,
        grid=(DEPTH, n_out // tn),
        in_specs=[
            pl.BlockSpec((rows, D_MODEL), lambda l, j: (0, 0)),
            pl.BlockSpec((1, D_MODEL, tn), lambda l, j: (l, 0, j)),
            pl.BlockSpec((1, 1, tn), lambda l, j: (l, 0, j)),
        ],
        out_specs=pl.BlockSpec((1, rows, tn), lambda l, j: (l, 0, j)),
        out_shape=jax.ShapeDtypeStruct((DEPTH, rows, n_out), F32),
        compiler_params=_cparams(("arbitrary", "arbitrary")),
        name="adaln_mod",
    )(c_pad, w_ada, b_ada.reshape(DEPTH, 1, n_out))
    return out[:, :B].reshape(DEPTH, B, N_MOD, D_MODEL)


def _unit_dilation(unit):
    return DIL_GROUPS[unit // U_GROUP][1] if unit < U_A_END else 1


def _inproj_kernel(x_ref, mod_ref, w_ref, o_ref, h_ref, slab_ref, *, tm, tn):
    j = pl.program_id(1)

    @pl.when(j == 0)
    def _():
        shift = mod_ref[0, 0:1, :]
        scale = mod_ref[0, 1:2, :]
        h_ref[...] = (x_ref[...] * (1.0 + scale) + shift).astype(BF16)

    acc = _dot(h_ref[...], w_ref[...])
    parts = tn // W4
    tile_dils = [[_unit_dilation((jj * tn + p * W4) // LANE) for p in range(parts)] for jj in range(N_PROJ // tn)]
    perm_tiles = [jj for jj, ds in enumerate(tile_dils) if any(d > 1 for d in ds)]
    is_perm = functools.reduce(jnp.logical_or, [j == jj for jj in perm_tiles])

    @pl.when(jnp.logical_not(is_perm))
    def _():
        o_ref[...] = acc.astype(o_ref.dtype)

    for jj in perm_tiles:
        @pl.when(j == jj)
        def _(jj=jj):
            for p, dil in enumerate(tile_dils[jj]):
                c0 = p * W4
                if dil == 1:
                    o_ref[:, c0:c0 + W4] = acc[:, c0:c0 + W4].astype(o_ref.dtype)
                    continue
                n = tm // dil
                for s in range(W4 // LANE):
                    slab_ref[s] = acc[:, c0 + s * LANE:c0 + (s + 1) * LANE]
                for s in range(W4 // LANE):
                    for r in range(dil):
                        o_ref[r * n:(r + 1) * n, c0 + s * LANE:c0 + (s + 1) * LANE] = (
                            slab_ref[s, pl.ds(r, n, stride=dil), :].astype(o_ref.dtype))


def _in_projection(x2, mod_l, w_cat, seq):
    T = x2.shape[0]
    tm = PERM_TILE
    tn = 1024
    per_b = seq // tm
    return pl.pallas_call(
        functools.partial(_inproj_kernel, tm=tm, tn=tn),
        grid=(T // tm, N_PROJ // tn),
        in_specs=[
            pl.BlockSpec((tm, D_MODEL), lambda i, j: (i, 0)),
            pl.BlockSpec((1, N_MOD, D_MODEL), lambda i, j: (i // per_b, 0, 0)),
            pl.BlockSpec((D_MODEL, tn), lambda i, j: (0, j)),
        ],
        out_specs=pl.BlockSpec((tm, tn), lambda i, j: (i, j)),
        out_shape=jax.ShapeDtypeStruct((T, N_PROJ), BF16),
        scratch_shapes=[pltpu.VMEM((tm, D_MODEL), BF16), pltpu.VMEM((W4 // LANE, tm, LANE), F32)],
        compiler_params=_cparams(("parallel", "arbitrary")),
        name="in_projection",
    )(x2, mod_l, w_cat)


LSE_LANES = LANE // HEADS_PER_GROUP


def _attn_kernel(*refs, c, nt, bias_scale, has_prev):
    if has_prev:
        q_ref, kp_ref, k_ref, vp_ref, v_ref, o_ref, lse_ref = refs
        first_run = pl.program_id(1) == 0
    else:
        q_ref, k_ref, v_ref, o_ref, lse_ref = refs
    qi = lax.broadcasted_iota(jnp.int32, (BAND, BAND), 0)
    ki = lax.broadcasted_iota(jnp.int32, (BAND, BAND), 1)
    valid_prev = ki >= qi
    valid_cur = ki <= qi
    delta_prev = (qi + BAND - ki).astype(F32)
    delta_cur = (qi - ki).astype(F32)
    neg_inf = jnp.float32(-jnp.inf)
    sm_scale = HEAD_DIM ** -0.5

    def rows(sb):
        if c >= BAND:
            per = c // BAND
            return sb // per, slice((sb % per) * BAND, (sb % per + 1) * BAND)
        per = BAND // c
        return slice(sb * per, (sb + 1) * per), slice(None)

    def load(ref, sb, cs):
        t, r = rows(sb)
        return ref[t, r, cs].reshape(BAND, cs.stop - cs.start)

    def store(ref, sb, cs, val):
        t, r = rows(sb)
        if c < BAND:
            val = val.reshape(BAND // c, c, cs.stop - cs.start)
        ref[t, r, cs] = val

    for hh in range(HEADS_PER_GROUP):
        cs = slice(hh * HEAD_DIM, (hh + 1) * HEAD_DIM)
        slope = bias_scale[hh]
        for sb in range(nt * c // BAND):
            q = load(q_ref, sb, cs)
            k_cur, v_cur = load(k_ref, sb, cs), load(v_ref, sb, cs)
            s_cur = jnp.where(valid_cur, _dot_nt(q, k_cur) * sm_scale - slope * delta_cur, neg_inf)
            m = jnp.max(s_cur, axis=1, keepdims=True)
            if sb > 0 or has_prev:
                if sb > 0:
                    k_prev, v_prev, mask = load(k_ref, sb - 1, cs), load(v_ref, sb - 1, cs), valid_prev
                else:
                    k_prev, v_prev = kp_ref[0, :, cs], vp_ref[0, :, cs]
                    mask = valid_prev & jnp.logical_not(first_run)
                s_prev = jnp.where(mask, _dot_nt(q, k_prev) * sm_scale - slope * delta_prev, neg_inf)
                m = jnp.maximum(m, jnp.max(s_prev, axis=1, keepdims=True))
                p_prev = jnp.exp(s_prev - m)
            p_cur = jnp.exp(s_cur - m)
            den = jnp.sum(p_cur, axis=1, keepdims=True)
            o = _dot(p_cur.astype(BF16), v_cur)
            if sb > 0 or has_prev:
                den = den + jnp.sum(p_prev, axis=1, keepdims=True)
                o = o + _dot(p_prev.astype(BF16), v_prev)
            store(o_ref, sb, cs, o / den)
            store(lse_ref, sb, slice(hh * LSE_LANES, (hh + 1) * LSE_LANES),
                  jnp.broadcast_to(m + jnp.log(den), (BAND, LSE_LANES)))


def _dilated_group(proj, batch, seq, g):
    _, dil = DIL_GROUPS[g]
    T = batch * seq
    tiles = T // PERM_TILE
    per_b = seq // PERM_TILE
    qcol, kcol, vcol = (g * U_GROUP * LANE // W4 + i for i in range(3))
    slopes = [2.0 ** (-ALIBI_MAX_EXP * (g * HEADS_PER_GROUP + h + 1) / A_HEADS) * dil
              for h in range(HEADS_PER_GROUP)]
    if dil == 1:
        c, nt = PERM_TILE, 1
        runs = proj.reshape(tiles, PERM_TILE, N_PROJ)
        bands = proj.reshape(T // BAND, BAND, N_PROJ)
        per_run = PERM_TILE // BAND
        cur = lambda col: pl.BlockSpec((1, c, W4), lambda b, n: (b * per_b + n, 0, col))
        prev = lambda col: pl.BlockSpec(
            (1, BAND, W4), lambda b, n: (jnp.maximum((b * per_b + n) * per_run - 1, 0), 0, col))
        grid = (batch, per_b)
        in_specs = [cur(qcol), prev(kcol), cur(kcol), prev(vcol), cur(vcol)]
        args = (runs, bands, runs, bands, runs)
        out_specs = [pl.BlockSpec((1, c, W4), lambda b, n: (b * per_b + n, 0, 0)),
                     pl.BlockSpec((1, c, LANE), lambda b, n: (b * per_b + n, 0, 0))]
        out_shape = [jax.ShapeDtypeStruct((tiles, c, W4), F32), jax.ShapeDtypeStruct((tiles, c, LANE), F32)]
        semantics = ("parallel", "arbitrary")
    else:
        c, nt = PERM_TILE // dil, per_b
        runs = proj.reshape(tiles, dil, c, N_PROJ)
        cur = lambda col: pl.BlockSpec((nt, None, c, W4), lambda b, r: (b, r, 0, col))
        grid = (batch, dil)
        in_specs = [cur(qcol), cur(kcol), cur(vcol)]
        args = (runs, runs, runs)
        out_specs = [pl.BlockSpec((nt, None, c, W4), lambda b, r: (b, r, 0, 0)),
                     pl.BlockSpec((nt, None, c, LANE), lambda b, r: (b, r, 0, 0))]
        out_shape = [jax.ShapeDtypeStruct((tiles, dil, c, W4), F32),
                     jax.ShapeDtypeStruct((tiles, dil, c, LANE), F32)]
        semantics = ("parallel", "parallel")
    o, lse = pl.pallas_call(
        functools.partial(_attn_kernel, c=c, nt=nt, bias_scale=slopes, has_prev=dil == 1),
        grid=grid,
        in_specs=in_specs,
        out_specs=out_specs,
        out_shape=out_shape,
        compiler_params=_cparams(semantics),
        name=f"dilated_attn_g{g}",
    )(*args)
    return o.reshape(T, W4), lse.reshape(T, LANE)


GLA_GROUP = 256


def _gla_kernel(q_ref, k_ref, v_ref, r_ref, low_ref, whi_ref, wlo_ref, balpha_ref, gnorm_ref,
                y_ref, state_ref, qe_ref, ke_ref, oi_ref, *, tb):
    @pl.when(pl.program_id(2) == 0)
    def _():
        state_ref[...] = jnp.zeros_like(state_ref)

    C = GLA_CHUNK
    G = GLA_GROUP
    low = low_ref[...]
    z = _dot(low, whi_ref[...]) + _dot(low, wlo_ref[...]) + balpha_ref[...]
    log_a = (jnp.minimum(z, 0.0) - jnp.log(1.0 + jnp.exp(-jnp.abs(z)))) * (1.0 / GLA_TAU)

    ri = lax.broadcasted_iota(jnp.int32, (G, G), 0)
    ci = lax.broadcasted_iota(jnp.int32, (G, G), 1)
    same_chunk = (ri // C) == (ci // C)
    causal = same_chunk & (ci <= ri)
    tril = causal.astype(BF16)
    sel_r = lax.broadcasted_iota(jnp.int32, (G, (G // C) * LANE), 0)
    sel_c = lax.broadcasted_iota(jnp.int32, (G, (G // C) * LANE), 1)
    chunk_sel = ((sel_r // C) == (sel_c // LANE)).astype(BF16)

    for grp in range(tb // G):
        rows = slice(grp * G, (grp + 1) * G)
        la = log_a[rows]
        la_hi = la.astype(BF16)
        la_lo = (la - la_hi.astype(F32)).astype(BF16)
        gc = _dot(tril, la_hi) + _dot(tril, la_lo)
        gtot_cols = _dot_tn(la_hi, chunk_sel) + _dot_tn(la_lo, chunk_sel)
        q_g = q_ref[rows, :].astype(F32) * (GLA_HK ** -0.5)
        k_g = k_ref[rows, :].astype(F32)
        for c in range(G // C):
            cr = slice(c * C, (c + 1) * C)
            gcc = gc[cr]
            g_ref_pt = gcc[C // 2 - 1:C // 2]
            g_tot = gcc[C - 1:C]
            qc, kc = q_g[cr], k_g[cr]
            qe_ref[cr, :] = (qc * jnp.exp(gcc - g_ref_pt)).astype(BF16)
            ke_ref[cr, :] = (kc * jnp.exp(g_ref_pt - gcc)).astype(BF16)
            q_in = (qc * jnp.exp(gcc)).astype(BF16)
            k_out = (kc * jnp.exp(g_tot - gcc)).astype(BF16)
            state = state_ref[...]
            oi_ref[cr, :] = _dot(q_in, state.astype(BF16))
            decay = jnp.exp(gtot_cols[:, c * LANE:(c + 1) * LANE])
            upd = _dot_tn(k_out, v_ref[grp * G + c * C:grp * G + (c + 1) * C, :])
            state_ref[...] = jnp.concatenate(
                [state[:, j * LANE:(j + 1) * LANE] * decay for j in range(GLA_HV // LANE)], axis=1) + upd
        att = _dot_nt(qe_ref[...], ke_ref[...])
        att = jnp.where(causal, att, 0.0).astype(BF16)
        o = _dot(att, v_ref[rows, :]) + oi_ref[...]
        ms = jnp.mean(o * o, axis=-1, keepdims=True)
        o = o * lax.rsqrt(ms + NORM_EPS) * gnorm_ref[...]
        y_ref[rows, :] = (_silu(r_ref[rows, :].astype(F32)) * o).astype(y_ref.dtype)


def _gla_branch(proj, batch, seq, w_up_hi, w_up_lo, b_alpha, gnorm):
    T = batch * seq
    tb = 512
    per_b = seq // tb

    def rowcol(unit, width):
        return lambda b, h, n: (b * per_b + n, unit * LANE // width + h)

    return pl.pallas_call(
        functools.partial(_gla_kernel, tb=tb),
        grid=(batch, GLA_HEADS, per_b),
        in_specs=[
            pl.BlockSpec((tb, GLA_HK), rowcol(U_BQ, GLA_HK)),
            pl.BlockSpec((tb, GLA_HK), rowcol(U_BK, GLA_HK)),
            pl.BlockSpec((tb, GLA_HV), rowcol(U_BV, GLA_HV)),
            pl.BlockSpec((tb, GLA_HV), rowcol(U_BR, GLA_HV)),
            pl.BlockSpec((tb, LANE), lambda b, h, n: (b * per_b + n, U_LOW)),
            pl.BlockSpec((LANE, GLA_HK), lambda b, h, n: (0, h)),
            pl.BlockSpec((LANE, GLA_HK), lambda b, h, n: (0, h)),
            pl.BlockSpec((1, GLA_HK), lambda b, h, n: (0, h)),
            pl.BlockSpec((1, GLA_HV), lambda b, h, n: (0, 0)),
        ],
        out_specs=pl.BlockSpec((tb, GLA_HV), lambda b, h, n: (b * per_b + n, h)),
        out_shape=jax.ShapeDtypeStruct((T, GLA_DV), BF16),
        scratch_shapes=[
            pltpu.VMEM((GLA_HK, GLA_HV), F32),
            pltpu.VMEM((GLA_GROUP, GLA_HK), BF16),
            pltpu.VMEM((GLA_GROUP, GLA_HK), BF16),
            pltpu.VMEM((GLA_GROUP, GLA_HV), F32),
        ],
        compiler_params=_cparams(("parallel", "parallel", "arbitrary")),
        name="gla_branch",
    )(proj, proj, proj, proj, proj, w_up_hi, w_up_lo, b_alpha.reshape(1, GLA_DK), gnorm.reshape(1, GLA_HV))


def _branch_kernel(o0_ref, o1_ref, o2_ref, l0_ref, l1_ref, l2_ref, yb_ref, ga_ref, gb_ref, bm_ref,
                   wa_ref, wb_ref, out_ref, ya_ref, oslab_ref, lslab_ref, *, tm):
    @pl.when(pl.program_id(1) == 0)
    def _():
        for gi, (o_ref, l_ref) in enumerate(((o1_ref, l1_ref), (o2_ref, l2_ref))):
            dil = DIL_GROUPS[gi + 1][1]
            n = tm // dil
            for r in range(dil):
                src = slice(r * n, (r + 1) * n)
                dst = pl.ds(r, n, stride=dil)
                for h in range(HEADS_PER_GROUP):
                    oslab_ref[gi, h, dst, :] = o_ref[src, h * HEAD_DIM:(h + 1) * HEAD_DIM]
                lslab_ref[gi, dst, :] = l_ref[src, :]
        l0, l1, l2 = l0_ref[...], lslab_ref[0], lslab_ref[1]
        m = jnp.maximum(jnp.maximum(l0, l1), l2)
        e0, e1, e2 = jnp.exp(l0 - m), jnp.exp(l1 - m), jnp.exp(l2 - m)
        inv = 1.0 / (e0 + e1 + e2)
        w0, w1, w2 = e0 * inv, e1 * inv, e2 * inv
        for h in range(HEADS_PER_GROUP):
            cs = slice(h * HEAD_DIM, (h + 1) * HEAD_DIM)
            lane = slice(h * LSE_LANES, h * LSE_LANES + 1)
            ya = o0_ref[:, cs] * w0[:, lane] + oslab_ref[0, h] * w1[:, lane] + oslab_ref[1, h] * w2[:, lane]
            ya_ref[:, cs] = ya.astype(BF16)

    pa = _dot(ya_ref[...], wa_ref[...])
    pb = _dot(yb_ref[...], wb_ref[...])
    gate_a = _sigmoid(ga_ref[...].astype(F32) + bm_ref[0:1, :])
    gate_b = _sigmoid(gb_ref[...].astype(F32) + bm_ref[1:2, :])
    out_ref[...] = (gate_a * pa + gate_b * pb).astype(out_ref.dtype)


def _branch_mix(outs, lses, y_b, proj, b_merge, wa, wb):
    T = y_b.shape[0]
    tm, tn = PERM_TILE, 512
    row = lambda i, j: (i, 0)
    n_dilated = len(DIL_GROUPS) - 1
    return pl.pallas_call(
        functools.partial(_branch_kernel, tm=tm),
        grid=(T // tm, D_MODEL // tn),
        in_specs=[pl.BlockSpec((tm, A_OUT), row)] * 3 + [pl.BlockSpec((tm, LANE), row)] * 3 + [
            pl.BlockSpec((tm, GLA_DV), row),
            pl.BlockSpec((tm, tn), lambda i, j: (i, U_GA * LANE // tn + j)),
            pl.BlockSpec((tm, tn), lambda i, j: (i, U_GB * LANE // tn + j)),
            pl.BlockSpec((N_BRANCH, tn), lambda i, j: (0, j)),
            pl.BlockSpec((A_OUT, tn), lambda i, j: (0, j)),
            pl.BlockSpec((GLA_DV, tn), lambda i, j: (0, j)),
        ],
        out_specs=pl.BlockSpec((tm, tn), lambda i, j: (i, j)),
        out_shape=jax.ShapeDtypeStruct((T, D_MODEL), BF16),
        scratch_shapes=[
            pltpu.VMEM((tm, A_OUT), BF16),
            pltpu.VMEM((n_dilated, HEADS_PER_GROUP, tm, HEAD_DIM), F32),
            pltpu.VMEM((n_dilated, tm, LANE), F32),
        ],
        compiler_params=_cparams(("parallel", "arbitrary")),
        name="branch_mix",
    )(*outs, *lses, y_b, proj, proj, b_merge, wa, wb)


def _outproj_kernel(mixed_ref, x_ref, mod_ref, w_ref, g_ref, b_ref, *out_refs, emit_h):
    y = _dot(mixed_ref[...], w_ref[...])
    gate = mod_ref[0, 2:3, :]
    xn = _layer_norm(DN_ALPHA * x_ref[...] + gate * y, g_ref[...], b_ref[...])
    out_refs[0][...] = xn
    if emit_h:
        out_refs[1][...] = xn * (1.0 + mod_ref[0, 4:5, :]) + mod_ref[0, 3:4, :]


def _out_projection(mixed, x2, mod_l, w_out, ln_g, ln_b, seq, emit_h):
    T = x2.shape[0]
    tm = 256
    per_b = seq // tm
    row = pl.BlockSpec((tm, D_MODEL), lambda i: (i, 0))
    vec = pl.BlockSpec((1, D_MODEL), lambda i: (0, 0))
    n_out = 2 if emit_h else 1
    outs = pl.pallas_call(
        functools.partial(_outproj_kernel, emit_h=emit_h),
        grid=(T // tm,),
        in_specs=[
            row, row,
            pl.BlockSpec((1, N_MOD, D_MODEL), lambda i: (i // per_b, 0, 0)),
            pl.BlockSpec((D_MODEL, D_MODEL), lambda i: (0, 0)),
            vec, vec,
        ],
        out_specs=[row] * n_out,
        out_shape=[jax.ShapeDtypeStruct((T, D_MODEL), F32)] * n_out,
        compiler_params=_cparams(("parallel",)),
        name="out_projection_ln",
    )(mixed, x2, mod_l, w_out, ln_g.reshape(1, D_MODEL), ln_b.reshape(1, D_MODEL))
    return outs if emit_h else (outs[0], None)


def _ffn_kernel(x_ref, mod_ref, w1_ref, w3_ref, w2_ref, g_ref, b_ref, o_ref, h_ref, acc_ref):
    f = pl.program_id(1)

    @pl.when(f == 0)
    def _():
        h_ref[...] = (x_ref[...] * (1.0 + mod_ref[0, 4:5, :]) + mod_ref[0, 3:4, :]).astype(BF16)
        acc_ref[...] = jnp.zeros_like(acc_ref)

    h = h_ref[...]
    act = (_silu(_dot(h, w1_ref[...])) * _dot(h, w3_ref[...])).astype(BF16)
    acc_ref[...] += _dot(act, w2_ref[...])

    @pl.when(f == pl.num_programs(1) - 1)
    def _():
        z = DN_ALPHA * x_ref[...] + mod_ref[0, 5:6, :] * acc_ref[...]
        o_ref[...] = _layer_norm(z, g_ref[...], b_ref[...])


def _dense_ffn(x2, mod_l, w1, w3, w2, ln_g, ln_b, seq):
    T = x2.shape[0]
    tm, tf = 512, 512
    per_b = seq // tm
    row = pl.BlockSpec((tm, D_MODEL), lambda i, f: (i, 0))
    vec = pl.BlockSpec((1, D_MODEL), lambda i, f: (0, 0))
    return pl.pallas_call(
        _ffn_kernel,
        grid=(T // tm, FFN_DENSE // tf),
        in_specs=[
            row,
            pl.BlockSpec((1, N_MOD, D_MODEL), lambda i, f: (i // per_b, 0, 0)),
            pl.BlockSpec((D_MODEL, tf), lambda i, f: (0, f)),
            pl.BlockSpec((D_MODEL, tf), lambda i, f: (0, f)),
            pl.BlockSpec((tf, D_MODEL), lambda i, f: (f, 0)),
            vec, vec,
        ],
        out_specs=row,
        out_shape=jax.ShapeDtypeStruct((T, D_MODEL), F32),
        scratch_shapes=[pltpu.VMEM((tm, D_MODEL), BF16), pltpu.VMEM((tm, D_MODEL), F32)],
        compiler_params=_cparams(("parallel", "arbitrary")),
        name="dense_ffn_ln",
    )(x2, mod_l, w1, w3, w2, ln_g.reshape(1, D_MODEL), ln_b.reshape(1, D_MODEL))


MOE_TF = 512
MOE_STEPS = FFN_EXPERT // MOE_TF
MOE_CHUNK = 40
MOE_TM = MOE_STEPS * MOE_CHUNK
INFO_I, INFO_W, INFO_R = 0, 2, 4


def _router_kernel(h_ref, wr_ref, info_ref, cnt_ref, carry_ref, *, tm):
    @pl.when(pl.program_id(0) == 0)
    def _():
        carry_ref[...] = jnp.zeros_like(carry_ref)

    logits = _dot(h_ref[...].astype(BF16), wr_ref[...])
    lane = lax.broadcasted_iota(jnp.int32, (tm, LANE), 1).astype(F32)
    neg_inf = jnp.float32(-jnp.inf)
    lg = jnp.where(lane < N_EXPERTS, logits, neg_inf)
    m1 = jnp.max(lg, axis=1, keepdims=True)
    i1 = jnp.min(jnp.where(lg == m1, lane, float(LANE)), axis=1, keepdims=True)
    lg2 = jnp.where(lane == i1, neg_inf, lg)
    m2 = jnp.max(lg2, axis=1, keepdims=True)
    i2 = jnp.min(jnp.where(lg2 == m2, lane, float(LANE)), axis=1, keepdims=True)
    e = jnp.exp(m2 - m1)
    w1 = 1.0 / (1.0 + e)
    w2 = e / (1.0 + e)
    oh1 = lane == i1
    oh2 = lane == i2
    onehot = (oh1 | oh2).astype(BF16)
    ri = lax.broadcasted_iota(jnp.int32, (tm, tm), 0)
    ci = lax.broadcasted_iota(jnp.int32, (tm, tm), 1)
    strict_lower = (ci < ri).astype(BF16)
    rank = _dot(strict_lower, onehot) + carry_ref[...]
    r1 = jnp.sum(jnp.where(oh1, rank, 0.0), axis=1, keepdims=True)
    r2 = jnp.sum(jnp.where(oh2, rank, 0.0), axis=1, keepdims=True)
    carry = carry_ref[...] + jnp.sum(onehot.astype(F32), axis=0, keepdims=True)
    carry_ref[...] = carry
    cnt_ref[...] = jnp.broadcast_to(carry, cnt_ref.shape)
    info = jnp.where(lane == INFO_I, i1, 0.0)
    info = jnp.where(lane == INFO_I + 1, i2, info)
    info = jnp.where(lane == INFO_W, w1, info)
    info = jnp.where(lane == INFO_W + 1, w2, info)
    info = jnp.where(lane == INFO_R, r1, info)
    info = jnp.where(lane == INFO_R + 1, r2, info)
    info_ref[...] = info


def _router(h2, w_router_pad):
    T = h2.shape[0]
    tm = 512
    return pl.pallas_call(
        functools.partial(_router_kernel, tm=tm),
        grid=(T // tm,),
        in_specs=[
            pl.BlockSpec((tm, D_MODEL), lambda i: (i, 0)),
            pl.BlockSpec((D_MODEL, LANE), lambda i: (0, 0)),
        ],
        out_specs=[
            pl.BlockSpec((tm, LANE), lambda i: (i, 0)),
            pl.BlockSpec((8, LANE), lambda i: (0, 0)),
        ],
        out_shape=[jax.ShapeDtypeStruct((T, LANE), F32), jax.ShapeDtypeStruct((8, LANE), F32)],
        scratch_shapes=[pltpu.VMEM((1, LANE), F32)],
        compiler_params=_cparams(("arbitrary",)),
        name="moe_router",
    )(h2, w_router_pad)


def _row_copy(src_hbm, src_row, dst_vmem, dst_row, sem):
    return pltpu.make_async_copy(src_hbm.at[pl.ds(src_row, 1), :], dst_vmem.at[pl.ds(dst_row, 1), :], sem)


def _expert_kernel(tile_expert_ref, n_tiles_ref, slot_token_ref, dest_ref, h_hbm, w1_ref, w3_ref, w2_ref,
                   y_hbm, rows_ref, hb_ref, acc_ref, stage_ref, gsems, ssems, *, tm, chunk):
    i = pl.program_id(0)
    f = pl.program_id(1)
    nf = pl.num_programs(1)
    n_live = n_tiles_ref[0]
    live = i < n_live
    cur = i % 2
    prev = 1 - cur

    def gather_chunk(tile, part, dst):
        for u in range(chunk):
            row = part * chunk + u
            _row_copy(h_hbm, slot_token_ref[tile * tm + row], rows_ref.at[dst], row, gsems.at[dst]).start()

    def scatter_copy(src, row, dest_row):
        return pltpu.make_async_copy(stage_ref.at[src, pl.ds(row, 1), :], y_hbm.at[pl.ds(dest_row, 1), :],
                                     ssems.at[src])

    def scatter_chunk(part):
        for u in range(chunk):
            row = part * chunk + u
            scatter_copy(prev, row, dest_ref[i * tm + row]).start()

    @pl.when((i == 0) & (f == 0))
    def _():
        stage_ref[1] = jnp.zeros(stage_ref.shape[1:], stage_ref.dtype)

        def issue(part, carry):
            gather_chunk(0, part, 0)
            return carry

        lax.fori_loop(0, nf, issue, 0)

    @pl.when((f == 0) & (i <= n_live))
    def _():
        def wait(r, carry):
            _row_copy(h_hbm, 0, rows_ref.at[cur], r, gsems.at[cur]).wait()
            return carry

        lax.fori_loop(0, tm, wait, 0, unroll=chunk)

    @pl.when((f == 0) & (i >= 1) & (i <= n_live + 1))
    def _():
        def wait(r, carry):
            scatter_copy(cur, r, 0).wait()
            return carry

        lax.fori_loop(0, tm, wait, 0, unroll=chunk)

    @pl.when(live & (f == 0))
    def _():
        hb_ref[...] = rows_ref[cur].astype(BF16)
        acc_ref[...] = jnp.zeros_like(acc_ref)

    @pl.when(live)
    def _():
        gather_chunk(i + 1, f, prev)
        scatter_chunk(f)
        h = hb_ref[...]
        act = (_silu(_dot(h, w1_ref[0])) * _dot(h, w3_ref[0])).astype(BF16)
        acc_ref[...] += _dot(act, w2_ref[0])

    @pl.when(i == n_live)
    def _():
        scatter_chunk(f)

    @pl.when(live & (f == nf - 1))
    def _():
        stage_ref[cur] = acc_ref[...]


def _expert_ffn(h2, tile_expert, n_tiles, slot_token, dest, w1, w3, w2, n_out_rows):
    tm, tf, nf = MOE_TM, MOE_TF, MOE_STEPS
    max_tiles = tile_expert.shape[0]

    def live_tile(i, nt):
        return jnp.minimum(i, nt[0] - 1)

    def f_idx(i, f, nt):
        return jnp.where(i < nt[0], f, nf - 1)

    grid_spec = pltpu.PrefetchScalarGridSpec(
        num_scalar_prefetch=4,
        grid=(max_tiles, nf),
        in_specs=[
            pl.BlockSpec(memory_space=pl.ANY),
            pl.BlockSpec((1, D_MODEL, tf), lambda i, f, te, nt, st, ds: (te[live_tile(i, nt)], 0, f_idx(i, f, nt))),
            pl.BlockSpec((1, D_MODEL, tf), lambda i, f, te, nt, st, ds: (te[live_tile(i, nt)], 0, f_idx(i, f, nt))),
            pl.BlockSpec((1, tf, D_MODEL), lambda i, f, te, nt, st, ds: (te[live_tile(i, nt)], f_idx(i, f, nt), 0)),
        ],
        out_specs=pl.BlockSpec(memory_space=pl.ANY),
        scratch_shapes=[
            pltpu.VMEM((2, tm, D_MODEL), F32),
            pltpu.VMEM((tm, D_MODEL), BF16),
            pltpu.VMEM((tm, D_MODEL), F32),
            pltpu.VMEM((2, tm, D_MODEL), F32),
            pltpu.SemaphoreType.DMA((2,)),
            pltpu.SemaphoreType.DMA((2,)),
        ],
    )
    return pl.pallas_call(
        functools.partial(_expert_kernel, tm=tm, chunk=MOE_CHUNK),
        grid_spec=grid_spec,
        out_shape=jax.ShapeDtypeStruct((n_out_rows, D_MODEL), F32),
        compiler_params=_cparams(("arbitrary", "arbitrary")),
        name="moe_experts",
    )(tile_expert, n_tiles, slot_token, dest, h2, w1, w3, w2)


def _combine_kernel(y1_ref, y2_ref, x_ref, info_ref, mod_ref, g_ref, b_ref, o_ref):
    info = info_ref[...]
    y = info[:, INFO_W:INFO_W + 1] * y1_ref[...] + info[:, INFO_W + 1:INFO_W + 2] * y2_ref[...]
    z = DN_ALPHA * x_ref[...] + mod_ref[0, 5:6, :] * y
    o_ref[...] = _layer_norm(z, g_ref[...], b_ref[...])


def _moe_combine(y_tok, x2, info, mod_l, ln_g, ln_b, seq):
    T = x2.shape[0]
    tm = 512
    per_b = seq // tm
    row = pl.BlockSpec((tm, D_MODEL), lambda i: (i, 0))
    vec = pl.BlockSpec((1, D_MODEL), lambda i: (0, 0))
    return pl.pallas_call(
        _combine_kernel,
        grid=(T // tm,),
        in_specs=[
            row,
            pl.BlockSpec((tm, D_MODEL), lambda i: (T // tm + i, 0)),
            row,
            pl.BlockSpec((tm, LANE), lambda i: (i, 0)),
            pl.BlockSpec((1, N_MOD, D_MODEL), lambda i: (i // per_b, 0, 0)),
            vec, vec,
        ],
        out_specs=row,
        out_shape=jax.ShapeDtypeStruct((T, D_MODEL), F32),
        compiler_params=_cparams(("parallel",)),
        name="moe_combine_ln",
    )(y_tok, y_tok, x2, info, mod_l, ln_g.reshape(1, D_MODEL), ln_b.reshape(1, D_MODEL))


def _moe_sublayer(x2, h2, mod_l, w_router, w1, w3, w2, ln_g, ln_b, seq):
    T = x2.shape[0]
    tm = MOE_TM
    wr_pad = jnp.zeros((D_MODEL, LANE), BF16).at[:, :N_EXPERTS].set(w_router.astype(BF16))
    info, cnt = _router(h2, wr_pad)
    experts = info[:, INFO_I:INFO_I + TOP_K].astype(jnp.int32)
    ranks = info[:, INFO_R:INFO_R + TOP_K].astype(jnp.int32)
    counts = cnt[0, :N_EXPERTS].astype(jnp.int32)
    tiles_e = (counts + tm - 1) // tm
    tile_end = jnp.cumsum(tiles_e)
    tile_start = tile_end - tiles_e
    max_tiles = (T * TOP_K + N_EXPERTS * (tm - 1)) // tm + 2
    slots = (tile_start * tm)[experts] + ranks
    tile_ids = jnp.arange(max_tiles, dtype=jnp.int32)
    tile_expert = jnp.minimum(jnp.sum(tile_ids[:, None] >= tile_end[None, :], axis=1), N_EXPERTS - 1).astype(jnp.int32)
    n_tiles = tile_end[-1:].astype(jnp.int32)
    token_ids = jnp.broadcast_to(jnp.arange(T, dtype=jnp.int32)[:, None], (T, TOP_K))
    n_slots = max_tiles * tm
    spare = TOP_K * T + jnp.arange(n_slots, dtype=jnp.int32) % tm
    out_rows = jnp.arange(TOP_K, dtype=jnp.int32)[None, :] * T + token_ids
    slot_dest = spare.at[slots.reshape(-1)].set(out_rows.reshape(-1))
    slot_token = slot_dest % T
    dest = jnp.concatenate([spare[:tm], slot_dest[:n_slots - tm]])
    y_tok = _expert_ffn(h2, tile_expert, n_tiles, slot_token, dest, w1, w3, w2, TOP_K * T + tm)
    return _moe_combine(y_tok, x2, info, mod_l, ln_g, ln_b, seq)


def _pack_w_in(w_in_l):
    low = jnp.zeros((D_MODEL, N_PROJ - U_LOW * LANE), w_in_l.dtype).at[:, :GLA_RANK].set(w_in_l[:, _REF_LOW0:_REF_G0])
    a_cols = [w_in_l[:, part * A_WIDTH + g * W4:part * A_WIDTH + (g + 1) * W4]
              for g in range(len(DIL_GROUPS)) for part in range(3)]
    return jnp.concatenate(a_cols + [w_in_l[:, 3 * A_WIDTH:_REF_LOW0], w_in_l[:, _REF_G0:], low], axis=1).astype(BF16)


def _pack_w_up(w_up_l):
    pad = jnp.zeros((LANE, GLA_DK), F32).at[:GLA_RANK].set(w_up_l)
    hi = pad.astype(BF16)
    lo = (pad - hi.astype(F32)).astype(BF16)
    return hi, lo


def kernel(x, c, w_ada, b_ada, ln_g, ln_b, w_in, w_alpha_up, b_alpha, gla_norm_g, b_merge, w_branch_a,
           w_branch_b, w_out, ffn_w1, ffn_w3, ffn_w2, w_router, moe_w1, moe_w3, moe_w2):
    batch, seq, _ = x.shape
    T = batch * seq
    x2 = x.reshape(T, D_MODEL)
    mod = _adaln_mod(c, w_ada, b_ada)
    for l in range(DEPTH):
        mod_l = mod[l]
        proj = _in_projection(x2, mod_l, _pack_w_in(w_in[l]), seq)
        groups = [_dilated_group(proj, batch, seq, g) for g in range(len(DIL_GROUPS))]
        w_up_hi, w_up_lo = _pack_w_up(w_alpha_up[l])
        y_b = _gla_branch(proj, batch, seq, w_up_hi, w_up_lo, b_alpha[l], gla_norm_g[l])
        mixed = _branch_mix([g[0] for g in groups], [g[1] for g in groups], y_b, proj, b_merge[l],
                            w_branch_a[l].astype(BF16), w_branch_b[l].astype(BF16))
        moe_layer = l % 2 == 1
        x2, h2 = _out_projection(mixed, x2, mod_l, w_out[l].astype(BF16), ln_g[l, 0], ln_b[l, 0], seq, moe_layer)
        if moe_layer:
            e = l // 2
            x2 = _moe_sublayer(x2, h2, mod_l, w_router[e], moe_w1[e].astype(BF16), moe_w3[e].astype(BF16),
                               moe_w2[e].astype(BF16), ln_g[l, 1], ln_b[l, 1], seq)
        else:
            e = l // 2
            x2 = _dense_ffn(x2, mod_l, ffn_w1[e].astype(BF16), ffn_w3[e].astype(BF16), ffn_w2[e].astype(BF16),
                            ln_g[l, 1], ln_b[l, 1], seq)
    return x2.reshape(batch, seq, D_MODEL)
```

```python
import functools

import jax
import jax.numpy as jnp
from jax import lax
from jax.experimental import pallas as pl
from jax.experimental.pallas import tpu as pltpu

F32 = jnp.float32
BF16 = jnp.bfloat16

D_MODEL = 2048
DEPTH = 2
HEAD_DIM = 128
DIL_GROUPS = ((128, 1), (512, 4), (2048, 16))
HEADS_PER_GROUP = 4
A_HEADS = HEADS_PER_GROUP * len(DIL_GROUPS)
A_WIDTH = A_HEADS * HEAD_DIM
A_OUT = HEADS_PER_GROUP * HEAD_DIM
ALIBI_MAX_EXP = 8.0
GLA_HEADS = 4
GLA_DK = D_MODEL // 2
GLA_DV = D_MODEL
GLA_HK = GLA_DK // GLA_HEADS
GLA_HV = GLA_DV // GLA_HEADS
GLA_RANK = 16
GLA_TAU = 16.0
GLA_CHUNK = 64
N_BRANCH = 2
FFN_DENSE = 5632
N_EXPERTS = 8
TOP_K = 2
FFN_EXPERT = 7168
N_MOD = 6
LN_EPS = 1e-5
NORM_EPS = 1e-6
DN_ALPHA = (2 * DEPTH) ** 0.25

LANE = 128
MXU_DIM = 256
VMEM_LIMIT_MB = 56

U_GROUP = 12
U_A_END = 36
U_BQ, U_BK, U_BV, U_BR = 36, 44, 52, 68
U_GA, U_GB = 84, 100
U_LOW = 116
N_UNITS = 120
N_PROJ = N_UNITS * LANE
BAND = 128
W4 = HEADS_PER_GROUP * HEAD_DIM
PERM_TILE = 1024

_REF_LOW0 = 3 * A_WIDTH + 2 * GLA_DK + 2 * GLA_DV
_REF_G0 = _REF_LOW0 + GLA_RANK


def _cparams(semantics, vmem_mb=VMEM_LIMIT_MB):
    return pltpu.CompilerParams(dimension_semantics=semantics, vmem_limit_bytes=vmem_mb << 20)


def _dot(a, b):
    return jnp.dot(a, b, preferred_element_type=F32)


def _dot_nt(a, b):
    return lax.dot_general(a, b, (((1,), (1,)), ((), ())), preferred_element_type=F32)


def _dot_tn(a, b):
    return lax.dot_general(a, b, (((0,), (0,)), ((), ())), preferred_element_type=F32)


def _sigmoid(x):
    return 1.0 / (1.0 + jnp.exp(-x))


def _silu(x):
    return x * _sigmoid(x)


def _layer_norm(z, g, b):
    mu = jnp.mean(z, axis=-1, keepdims=True)
    zc = z - mu
    var = jnp.mean(zc * zc, axis=-1, keepdims=True)
    return zc * lax.rsqrt(var + LN_EPS) * g + b


def _mod_kernel(c_ref, w_ref, b_ref, o_ref):
    s = _silu(c_ref[...]).astype(BF16)
    o_ref[0] = _dot(s, w_ref[0].astype(BF16)) + b_ref[0]


def _adaln_mod(c, w_ada, b_ada):
    B = c.shape[0]
    rows = 8
    tn = 1024
    c_pad = jnp.zeros((rows, D_MODEL), F32).at[:B].set(c)
    n_out = N_MOD * D_MODEL
    out = pl.pallas_call(
        _mod_kernel,
        grid=(DEPTH, n_out // tn),
        in_specs=[
            pl.BlockSpec((rows, D_MODEL), lambda l, j: (0, 0)),
            pl.BlockSpec((1, D_MODEL, tn), lambda l, j: (l, 0, j)),
            pl.BlockSpec((1, 1, tn), lambda l, j: (l, 0, j)),
        ],
        out_specs=pl.BlockSpec((1, rows, tn), lambda l, j: (l, 0, j)),
        out_shape=jax.ShapeDtypeStruct((DEPTH, rows, n_out), F32),
        compiler_params=_cparams(("arbitrary", "arbitrary")),
        name="adaln_mod",
    )(c_pad, w_ada, b_ada.reshape(DEPTH, 1, n_out))
    return out[:, :B].reshape(DEPTH, B, N_MOD, D_MODEL)


def _unit_dilation(unit):
    return DIL_GROUPS[unit // U_GROUP][1] if unit < U_A_END else 1


def _side_cast_specs(src, block, steps_inner):
    n_e, n_r, n_c = src.shape
    _, br, bc = block
    per_e = (n_r // br) * (n_c // bc)
    last = n_e * per_e - 1

    def index(i, j):
        s = jnp.minimum(i * steps_inner + j, last)
        rem = s % per_e
        return s // per_e, rem // (n_c // bc), rem % (n_c // bc)

    return pl.BlockSpec(block, index), last + 1


def _inproj_kernel(*refs, tm, tn, side):
    if side:
        x_ref, mod_ref, w_ref, side_ref, o_ref, side_out_ref, h_ref, slab_ref = refs
        side_out_ref[...] = side_ref[...].astype(side_out_ref.dtype)
    else:
        x_ref, mod_ref, w_ref, o_ref, h_ref, slab_ref = refs
    j = pl.program_id(1)

    @pl.when(j == 0)
    def _():
        shift = mod_ref[0, 0:1, :]
        scale = mod_ref[0, 1:2, :]
        h_ref[...] = (x_ref[...] * (1.0 + scale) + shift).astype(BF16)

    acc = _dot(h_ref[...], w_ref[...])
    parts = tn // W4
    tile_dils = [[_unit_dilation((jj * tn + p * W4) // LANE) for p in range(parts)] for jj in range(N_PROJ // tn)]
    perm_tiles = [jj for jj, ds in enumerate(tile_dils) if any(d > 1 for d in ds)]
    is_perm = functools.reduce(jnp.logical_or, [j == jj for jj in perm_tiles])

    @pl.when(jnp.logical_not(is_perm))
    def _():
        o_ref[...] = acc.astype(o_ref.dtype)

    for jj in perm_tiles:
        @pl.when(j == jj)
        def _(jj=jj):
            for p, dil in enumerate(tile_dils[jj]):
                c0 = p * W4
                if dil == 1:
                    o_ref[:, c0:c0 + W4] = acc[:, c0:c0 + W4].astype(o_ref.dtype)
                    continue
                n = tm // dil
                for s in range(W4 // LANE):
                    slab_ref[s] = acc[:, c0 + s * LANE:c0 + (s + 1) * LANE]
                for s in range(W4 // LANE):
                    for r in range(dil):
                        o_ref[r * n:(r + 1) * n, c0 + s * LANE:c0 + (s + 1) * LANE] = (
                            slab_ref[s, pl.ds(r, n, stride=dil), :].astype(o_ref.dtype))


def _in_projection(x2, mod_l, w_cat, seq, side=None):
    T = x2.shape[0]
    tm = PERM_TILE
    tn = 1024
    per_b = seq // tm
    grid = (T // tm, N_PROJ // tn)
    in_specs = [
        pl.BlockSpec((tm, D_MODEL), lambda i, j: (i, 0)),
        pl.BlockSpec((1, N_MOD, D_MODEL), lambda i, j: (i // per_b, 0, 0)),
        pl.BlockSpec((D_MODEL, tn), lambda i, j: (0, j)),
    ]
    out_specs = [pl.BlockSpec((tm, tn), lambda i, j: (i, j))]
    out_shape = [jax.ShapeDtypeStruct((T, N_PROJ), BF16)]
    args = [x2, mod_l, w_cat]
    if side is not None:
        src, block = side
        spec, n_blocks = _side_cast_specs(src, block, grid[1])
        assert n_blocks <= grid[0] * grid[1]
        in_specs.append(spec)
        out_specs.append(spec)
        out_shape.append(jax.ShapeDtypeStruct(src.shape, BF16))
        args.append(src)
    outs = pl.pallas_call(
        functools.partial(_inproj_kernel, tm=tm, tn=tn, side=side is not None),
        grid=grid,
        in_specs=in_specs,
        out_specs=out_specs,
        out_shape=out_shape,
        scratch_shapes=[pltpu.VMEM((tm, D_MODEL), BF16), pltpu.VMEM((W4 // LANE, tm, LANE), F32)],
        compiler_params=_cparams(("arbitrary", "arbitrary")),
        name="in_projection",
    )(*args)
    return (outs[0], outs[1]) if side is not None else (outs[0], None)


LSE_LANES = LANE // HEADS_PER_GROUP


def _attn_kernel(*refs, c, nt, bias_scale, has_prev):
    if has_prev:
        q_ref, kp_ref, k_ref, vp_ref, v_ref, o_ref, lse_ref = refs
        first_run = pl.program_id(1) == 0
    else:
        q_ref, k_ref, v_ref, o_ref, lse_ref = refs
    qi = lax.broadcasted_iota(jnp.int32, (BAND, BAND), 0)
    ki = lax.broadcasted_iota(jnp.int32, (BAND, BAND), 1)
    valid_prev = ki >= qi
    valid_cur = ki <= qi
    delta_prev = (qi + BAND - ki).astype(F32)
    delta_cur = (qi - ki).astype(F32)
    neg_inf = jnp.float32(-jnp.inf)
    sm_scale = HEAD_DIM ** -0.5

    def rows(sb):
        if c >= BAND:
            per = c // BAND
            return sb // per, slice((sb % per) * BAND, (sb % per + 1) * BAND)
        per = BAND // c
        return slice(sb * per, (sb + 1) * per), slice(None)

    def load(ref, sb, cs):
        t, r = rows(sb)
        return ref[t, r, cs].reshape(BAND, cs.stop - cs.start)

    def store(ref, sb, cs, val):
        t, r = rows(sb)
        if c < BAND:
            val = val.reshape(BAND // c, c, cs.stop - cs.start)
        ref[t, r, cs] = val

    for hh in range(HEADS_PER_GROUP):
        cs = slice(hh * HEAD_DIM, (hh + 1) * HEAD_DIM)
        slope = bias_scale[hh]
        bias_cur = jnp.where(valid_cur, -slope * delta_cur, neg_inf)
        bias_prev = jnp.where(valid_prev, -slope * delta_prev, neg_inf)
        for sb in range(nt * c // BAND):
            q = load(q_ref, sb, cs)
            k_cur, v_cur = load(k_ref, sb, cs), load(v_ref, sb, cs)
            s_cur = _dot_nt(q, k_cur) * sm_scale + bias_cur
            m = jnp.max(s_cur, axis=1, keepdims=True)
            if sb > 0 or has_prev:
                if sb > 0:
                    k_prev, v_prev = load(k_ref, sb - 1, cs), load(v_ref, sb - 1, cs)
                    s_prev = _dot_nt(q, k_prev) * sm_scale + bias_prev
                else:
                    k_prev, v_prev = kp_ref[0, :, cs], vp_ref[0, :, cs]
                    s_prev = jnp.where(first_run, neg_inf, _dot_nt(q, k_prev) * sm_scale + bias_prev)
                m = jnp.maximum(m, jnp.max(s_prev, axis=1, keepdims=True))
                p_prev = jnp.exp(s_prev - m)
            p_cur = jnp.exp(s_cur - m)
            den = jnp.sum(p_cur, axis=1, keepdims=True)
            o = _dot(p_cur.astype(BF16), v_cur)
            if sb > 0 or has_prev:
                den = den + jnp.sum(p_prev, axis=1, keepdims=True)
                o = o + _dot(p_prev.astype(BF16), v_prev)
            store(o_ref, sb, cs, o / den)
            store(lse_ref, sb, slice(hh * LSE_LANES, (hh + 1) * LSE_LANES),
                  jnp.broadcast_to(m + jnp.log(den), (BAND, LSE_LANES)))


def _dilated_group(proj, batch, seq, g):
    _, dil = DIL_GROUPS[g]
    T = batch * seq
    tiles = T // PERM_TILE
    per_b = seq // PERM_TILE
    qcol, kcol, vcol = (g * U_GROUP * LANE // W4 + i for i in range(3))
    slopes = [2.0 ** (-ALIBI_MAX_EXP * (g * HEADS_PER_GROUP + h + 1) / A_HEADS) * dil
              for h in range(HEADS_PER_GROUP)]
    if dil == 1:
        c, nt = PERM_TILE, 1
        runs = proj.reshape(tiles, PERM_TILE, N_PROJ)
        bands = proj.reshape(T // BAND, BAND, N_PROJ)
        per_run = PERM_TILE // BAND
        cur = lambda col: pl.BlockSpec((1, c, W4), lambda b, n: (b * per_b + n, 0, col))
        prev = lambda col: pl.BlockSpec(
            (1, BAND, W4), lambda b, n: (jnp.maximum((b * per_b + n) * per_run - 1, 0), 0, col))
        grid = (batch, per_b)
        in_specs = [cur(qcol), prev(kcol), cur(kcol), prev(vcol), cur(vcol)]
        args = (runs, bands, runs, bands, runs)
        out_specs = [pl.BlockSpec((1, c, W4), lambda b, n: (b * per_b + n, 0, 0)),
                     pl.BlockSpec((1, c, LANE), lambda b, n: (b * per_b + n, 0, 0))]
        out_shape = [jax.ShapeDtypeStruct((tiles, c, W4), F32), jax.ShapeDtypeStruct((tiles, c, LANE), F32)]
        semantics = ("parallel", "arbitrary")
    else:
        c, nt = PERM_TILE // dil, per_b
        runs = proj.reshape(tiles, dil, c, N_PROJ)
        cur = lambda col: pl.BlockSpec((nt, None, c, W4), lambda b, r: (b, r, 0, col))
        grid = (batch, dil)
        in_specs = [cur(qcol), cur(kcol), cur(vcol)]
        args = (runs, runs, runs)
        out_specs = [pl.BlockSpec((nt, None, c, W4), lambda b, r: (b, r, 0, 0)),
                     pl.BlockSpec((nt, None, c, LANE), lambda b, r: (b, r, 0, 0))]
        out_shape = [jax.ShapeDtypeStruct((tiles, dil, c, W4), F32),
                     jax.ShapeDtypeStruct((tiles, dil, c, LANE), F32)]
        semantics = ("parallel", "parallel")
    o, lse = pl.pallas_call(
        functools.partial(_attn_kernel, c=c, nt=nt, bias_scale=slopes, has_prev=dil == 1),
        grid=grid,
        in_specs=in_specs,
        out_specs=out_specs,
        out_shape=out_shape,
        compiler_params=_cparams(semantics),
        name=f"dilated_attn_g{g}",
    )(*args)
    return o.reshape(T, W4), lse.reshape(T, LANE)


GLA_GROUP = 256


def _gla_kernel(q_ref, k_ref, v_ref, r_ref, low_ref, whi_ref, wlo_ref, balpha_ref, gnorm_ref,
                y_ref, state_ref, qe_ref, ke_ref, oi_ref, *, tb):
    @pl.when(pl.program_id(2) == 0)
    def _():
        state_ref[...] = jnp.zeros_like(state_ref)

    C = GLA_CHUNK
    G = GLA_GROUP
    low = low_ref[...]
    z = _dot(low, whi_ref[...]) + _dot(low, wlo_ref[...]) + balpha_ref[...]
    log_a = (jnp.minimum(z, 0.0) - jnp.log(1.0 + jnp.exp(-jnp.abs(z)))) * (1.0 / GLA_TAU)

    ri = lax.broadcasted_iota(jnp.int32, (G, G), 0)
    ci = lax.broadcasted_iota(jnp.int32, (G, G), 1)
    same_chunk = (ri // C) == (ci // C)
    causal = same_chunk & (ci <= ri)
    tril = causal.astype(BF16)
    sel_r = lax.broadcasted_iota(jnp.int32, (G, (G // C) * LANE), 0)
    sel_c = lax.broadcasted_iota(jnp.int32, (G, (G // C) * LANE), 1)
    chunk_sel = ((sel_r // C) == (sel_c // LANE)).astype(BF16)

    for grp in range(tb // G):
        rows = slice(grp * G, (grp + 1) * G)
        la = log_a[rows]
        la_hi = la.astype(BF16)
        la_lo = (la - la_hi.astype(F32)).astype(BF16)
        gc = _dot(tril, la_hi) + _dot(tril, la_lo)
        gtot_cols = _dot_tn(la_hi, chunk_sel) + _dot_tn(la_lo, chunk_sel)
        q_g = q_ref[rows, :].astype(F32) * (GLA_HK ** -0.5)
        k_g = k_ref[rows, :].astype(F32)
        for c in range(G // C):
            cr = slice(c * C, (c + 1) * C)
            gcc = gc[cr]
            g_ref_pt = gcc[C // 2 - 1:C // 2]
            g_tot = gcc[C - 1:C]
            qc, kc = q_g[cr], k_g[cr]
            qe_ref[cr, :] = (qc * jnp.exp(gcc - g_ref_pt)).astype(BF16)
            ke_ref[cr, :] = (kc * jnp.exp(g_ref_pt - gcc)).astype(BF16)
            q_in = (qc * jnp.exp(gcc)).astype(BF16)
            k_out = (kc * jnp.exp(g_tot - gcc)).astype(BF16)
            state = state_ref[...]
            oi_ref[cr, :] = _dot(q_in, state.astype(BF16))
            decay = jnp.exp(gtot_cols[:, c * LANE:(c + 1) * LANE])
            upd = _dot_tn(k_out, v_ref[grp * G + c * C:grp * G + (c + 1) * C, :])
            state_ref[...] = jnp.concatenate(
                [state[:, j * LANE:(j + 1) * LANE] * decay for j in range(GLA_HV // LANE)], axis=1) + upd
        att = _dot_nt(qe_ref[...], ke_ref[...])
        att = jnp.where(causal, att, 0.0).astype(BF16)
        o = _dot(att, v_ref[rows, :]) + oi_ref[...]
        ms = jnp.mean(o * o, axis=-1, keepdims=True)
        o = o * lax.rsqrt(ms + NORM_EPS) * gnorm_ref[...]
        y_ref[rows, :] = (_silu(r_ref[rows, :].astype(F32)) * o).astype(y_ref.dtype)


def _gla_branch(proj, batch, seq, w_up_hi, w_up_lo, b_alpha, gnorm):
    T = batch * seq
    tb = 512
    per_b = seq // tb

    def rowcol(unit, width):
        return lambda b, h, n: (b * per_b + n, unit * LANE // width + h)

    return pl.pallas_call(
        functools.partial(_gla_kernel, tb=tb),
        grid=(batch, GLA_HEADS, per_b),
        in_specs=[
            pl.BlockSpec((tb, GLA_HK), rowcol(U_BQ, GLA_HK)),
            pl.BlockSpec((tb, GLA_HK), rowcol(U_BK, GLA_HK)),
            pl.BlockSpec((tb, GLA_HV), rowcol(U_BV, GLA_HV)),
            pl.BlockSpec((tb, GLA_HV), rowcol(U_BR, GLA_HV)),
            pl.BlockSpec((tb, LANE), lambda b, h, n: (b * per_b + n, U_LOW)),
            pl.BlockSpec((LANE, GLA_HK), lambda b, h, n: (0, h)),
            pl.BlockSpec((LANE, GLA_HK), lambda b, h, n: (0, h)),
            pl.BlockSpec((1, GLA_HK), lambda b, h, n: (0, h)),
            pl.BlockSpec((1, GLA_HV), lambda b, h, n: (0, 0)),
        ],
        out_specs=pl.BlockSpec((tb, GLA_HV), lambda b, h, n: (b * per_b + n, h)),
        out_shape=jax.ShapeDtypeStruct((T, GLA_DV), BF16),
        scratch_shapes=[
            pltpu.VMEM((GLA_HK, GLA_HV), F32),
            pltpu.VMEM((GLA_GROUP, GLA_HK), BF16),
            pltpu.VMEM((GLA_GROUP, GLA_HK), BF16),
            pltpu.VMEM((GLA_GROUP, GLA_HV), F32),
        ],
        compiler_params=_cparams(("parallel", "parallel", "arbitrary")),
        name="gla_branch",
    )(proj, proj, proj, proj, proj, w_up_hi, w_up_lo, b_alpha.reshape(1, GLA_DK), gnorm.reshape(1, GLA_HV))


def _branch_kernel(o0_ref, o1_ref, o2_ref, l0_ref, l1_ref, l2_ref, yb_ref, ga_ref, gb_ref, bm_ref,
                   wa_ref, wb_ref, out_ref, ya_ref, oslab_ref, lslab_ref, *, tm):
    @pl.when(pl.program_id(1) == 0)
    def _():
        for gi, (o_ref, l_ref) in enumerate(((o1_ref, l1_ref), (o2_ref, l2_ref))):
            dil = DIL_GROUPS[gi + 1][1]
            n = tm // dil
            for r in range(dil):
                src = slice(r * n, (r + 1) * n)
                dst = pl.ds(r, n, stride=dil)
                for h in range(HEADS_PER_GROUP):
                    oslab_ref[gi, h, dst, :] = o_ref[src, h * HEAD_DIM:(h + 1) * HEAD_DIM]
                lslab_ref[gi, dst, :] = l_ref[src, :]
        l0, l1, l2 = l0_ref[...], lslab_ref[0], lslab_ref[1]
        m = jnp.maximum(jnp.maximum(l0, l1), l2)
        e0, e1, e2 = jnp.exp(l0 - m), jnp.exp(l1 - m), jnp.exp(l2 - m)
        inv = 1.0 / (e0 + e1 + e2)
        w0, w1, w2 = e0 * inv, e1 * inv, e2 * inv
        for h in range(HEADS_PER_GROUP):
            cs = slice(h * HEAD_DIM, (h + 1) * HEAD_DIM)
            lane = slice(h * LSE_LANES, h * LSE_LANES + 1)
            ya = o0_ref[:, cs] * w0[:, lane] + oslab_ref[0, h] * w1[:, lane] + oslab_ref[1, h] * w2[:, lane]
            ya_ref[:, cs] = ya.astype(BF16)

    pa = _dot(ya_ref[...], wa_ref[...])
    pb = _dot(yb_ref[...], wb_ref[...])
    gate_a = _sigmoid(ga_ref[...].astype(F32) + bm_ref[0:1, :])
    gate_b = _sigmoid(gb_ref[...].astype(F32) + bm_ref[1:2, :])
    out_ref[...] = (gate_a * pa + gate_b * pb).astype(out_ref.dtype)


def _branch_mix(outs, lses, y_b, proj, b_merge, wa, wb):
    T = y_b.shape[0]
    tm, tn = PERM_TILE, 512
    row = lambda i, j: (i, 0)
    n_dilated = len(DIL_GROUPS) - 1
    return pl.pallas_call(
        functools.partial(_branch_kernel, tm=tm),
        grid=(T // tm, D_MODEL // tn),
        in_specs=[pl.BlockSpec((tm, A_OUT), row)] * 3 + [pl.BlockSpec((tm, LANE), row)] * 3 + [
            pl.BlockSpec((tm, GLA_DV), row),
            pl.BlockSpec((tm, tn), lambda i, j: (i, U_GA * LANE // tn + j)),
            pl.BlockSpec((tm, tn), lambda i, j: (i, U_GB * LANE // tn + j)),
            pl.BlockSpec((N_BRANCH, tn), lambda i, j: (0, j)),
            pl.BlockSpec((A_OUT, tn), lambda i, j: (0, j)),
            pl.BlockSpec((GLA_DV, tn), lambda i, j: (0, j)),
        ],
        out_specs=pl.BlockSpec((tm, tn), lambda i, j: (i, j)),
        out_shape=jax.ShapeDtypeStruct((T, D_MODEL), BF16),
        scratch_shapes=[
            pltpu.VMEM((tm, A_OUT), BF16),
            pltpu.VMEM((n_dilated, HEADS_PER_GROUP, tm, HEAD_DIM), F32),
            pltpu.VMEM((n_dilated, tm, LANE), F32),
        ],
        compiler_params=_cparams(("parallel", "arbitrary")),
        name="branch_mix",
    )(*outs, *lses, y_b, proj, proj, b_merge, wa, wb)


def _outproj_kernel(mixed_ref, x_ref, mod_ref, w_ref, g_ref, b_ref, *out_refs, emit_h, tm, parts):
    gate = mod_ref[0, 2:3, :]
    for p in range(parts):
        rs = slice(p * (tm // parts), (p + 1) * (tm // parts))
        y = _dot(mixed_ref[rs, :], w_ref[...])
        xn = _layer_norm(DN_ALPHA * x_ref[rs, :] + gate * y, g_ref[...], b_ref[...])
        out_refs[0][rs, :] = xn
        if emit_h:
            out_refs[1][rs, :] = xn * (1.0 + mod_ref[0, 4:5, :]) + mod_ref[0, 3:4, :]


def _out_projection(mixed, x2, mod_l, w_out, ln_g, ln_b, seq, emit_h):
    T = x2.shape[0]
    tm, parts = 512, 2
    per_b = seq // tm
    row = pl.BlockSpec((tm, D_MODEL), lambda i: (i, 0))
    vec = pl.BlockSpec((1, D_MODEL), lambda i: (0, 0))
    n_out = 2 if emit_h else 1
    outs = pl.pallas_call(
        functools.partial(_outproj_kernel, emit_h=emit_h, tm=tm, parts=parts),
        grid=(T // tm,),
        in_specs=[
            row, row,
            pl.BlockSpec((1, N_MOD, D_MODEL), lambda i: (i // per_b, 0, 0)),
            pl.BlockSpec((D_MODEL, D_MODEL), lambda i: (0, 0), pipeline_mode=pl.Buffered(1)),
            vec, vec,
        ],
        out_specs=[row] * n_out,
        out_shape=[jax.ShapeDtypeStruct((T, D_MODEL), F32)] * n_out,
        compiler_params=_cparams(("parallel",)),
        name="out_projection_ln",
    )(mixed, x2, mod_l, w_out, ln_g.reshape(1, D_MODEL), ln_b.reshape(1, D_MODEL))
    return outs if emit_h else (outs[0], None)


def _ffn_kernel(*refs, side):
    if side:
        x_ref, mod_ref, w1_ref, w3_ref, w2_ref, g_ref, b_ref, side_ref, o_ref, side_out_ref, h_ref, acc_ref = refs
        side_out_ref[...] = side_ref[...].astype(side_out_ref.dtype)
    else:
        x_ref, mod_ref, w1_ref, w3_ref, w2_ref, g_ref, b_ref, o_ref, h_ref, acc_ref = refs
    f = pl.program_id(1)

    @pl.when(f == 0)
    def _():
        h_ref[...] = (x_ref[...] * (1.0 + mod_ref[0, 4:5, :]) + mod_ref[0, 3:4, :]).astype(BF16)
        acc_ref[...] = jnp.zeros_like(acc_ref)

    h = h_ref[...]
    act = (_silu(_dot(h, w1_ref[...])) * _dot(h, w3_ref[...])).astype(BF16)
    acc_ref[...] += _dot(act, w2_ref[...])

    @pl.when(f == pl.num_programs(1) - 1)
    def _():
        z = DN_ALPHA * x_ref[...] + mod_ref[0, 5:6, :] * acc_ref[...]
        o_ref[...] = _layer_norm(z, g_ref[...], b_ref[...])


def _dense_ffn(x2, mod_l, w1, w3, w2, ln_g, ln_b, seq, side=None):
    T = x2.shape[0]
    tm, tf = 512, 512
    per_b = seq // tm
    grid = (T // tm, FFN_DENSE // tf)
    row = pl.BlockSpec((tm, D_MODEL), lambda i, f: (i, 0))
    vec = pl.BlockSpec((1, D_MODEL), lambda i, f: (0, 0))
    in_specs = [
        row,
        pl.BlockSpec((1, N_MOD, D_MODEL), lambda i, f: (i // per_b, 0, 0)),
        pl.BlockSpec((D_MODEL, tf), lambda i, f: (0, f)),
        pl.BlockSpec((D_MODEL, tf), lambda i, f: (0, f)),
        pl.BlockSpec((tf, D_MODEL), lambda i, f: (f, 0)),
        vec, vec,
    ]
    out_specs = [row]
    out_shape = [jax.ShapeDtypeStruct((T, D_MODEL), F32)]
    args = [x2, mod_l, w1, w3, w2, ln_g.reshape(1, D_MODEL), ln_b.reshape(1, D_MODEL)]
    if side is not None:
        src, block = side
        spec, n_blocks = _side_cast_specs(src, block, grid[1])
        assert n_blocks <= grid[0] * grid[1]
        in_specs.append(spec)
        out_specs.append(spec)
        out_shape.append(jax.ShapeDtypeStruct(src.shape, BF16))
        args.append(src)
    outs = pl.pallas_call(
        functools.partial(_ffn_kernel, side=side is not None),
        grid=grid,
        in_specs=in_specs,
        out_specs=out_specs,
        out_shape=out_shape,
        scratch_shapes=[pltpu.VMEM((tm, D_MODEL), BF16), pltpu.VMEM((tm, D_MODEL), F32)],
        compiler_params=_cparams(("arbitrary", "arbitrary")),
        name="dense_ffn_ln",
    )(*args)
    return (outs[0], outs[1]) if side is not None else (outs[0], None)


MOE_TF = 512
MOE_STEPS = FFN_EXPERT // MOE_TF
MOE_CHUNK = 40
MOE_TM = MOE_STEPS * MOE_CHUNK
INFO_I, INFO_W, INFO_R = 0, 2, 4


def _router_kernel(h_ref, wr_ref, info_ref, cnt_ref, carry_ref, *, tm):
    @pl.when(pl.program_id(0) == 0)
    def _():
        carry_ref[...] = jnp.zeros_like(carry_ref)

    logits = _dot(h_ref[...].astype(BF16), wr_ref[...])
    lane = lax.broadcasted_iota(jnp.int32, (tm, LANE), 1).astype(F32)
    neg_inf = jnp.float32(-jnp.inf)
    lg = jnp.where(lane < N_EXPERTS, logits, neg_inf)
    m1 = jnp.max(lg, axis=1, keepdims=True)
    i1 = jnp.min(jnp.where(lg == m1, lane, float(LANE)), axis=1, keepdims=True)
    lg2 = jnp.where(lane == i1, neg_inf, lg)
    m2 = jnp.max(lg2, axis=1, keepdims=True)
    i2 = jnp.min(jnp.where(lg2 == m2, lane, float(LANE)), axis=1, keepdims=True)
    e = jnp.exp(m2 - m1)
    w1 = 1.0 / (1.0 + e)
    w2 = e / (1.0 + e)
    oh1 = lane == i1
    oh2 = lane == i2
    onehot = (oh1 | oh2).astype(BF16)
    ri = lax.broadcasted_iota(jnp.int32, (tm, tm), 0)
    ci = lax.broadcasted_iota(jnp.int32, (tm, tm), 1)
    strict_lower = (ci < ri).astype(BF16)
    rank = _dot(strict_lower, onehot) + carry_ref[...]
    r1 = jnp.sum(jnp.where(oh1, rank, 0.0), axis=1, keepdims=True)
    r2 = jnp.sum(jnp.where(oh2, rank, 0.0), axis=1, keepdims=True)
    carry = carry_ref[...] + jnp.sum(onehot.astype(F32), axis=0, keepdims=True)
    carry_ref[...] = carry
    cnt_ref[...] = jnp.broadcast_to(carry, cnt_ref.shape)
    info = jnp.where(lane == INFO_I, i1, 0.0)
    info = jnp.where(lane == INFO_I + 1, i2, info)
    info = jnp.where(lane == INFO_W, w1, info)
    info = jnp.where(lane == INFO_W + 1, w2, info)
    info = jnp.where(lane == INFO_R, r1, info)
    info = jnp.where(lane == INFO_R + 1, r2, info)
    info_ref[...] = info


def _router(h2, w_router_pad):
    T = h2.shape[0]
    tm = 512
    return pl.pallas_call(
        functools.partial(_router_kernel, tm=tm),
        grid=(T // tm,),
        in_specs=[
            pl.BlockSpec((tm, D_MODEL), lambda i: (i, 0)),
            pl.BlockSpec((D_MODEL, LANE), lambda i: (0, 0)),
        ],
        out_specs=[
            pl.BlockSpec((tm, LANE), lambda i: (i, 0)),
            pl.BlockSpec((8, LANE), lambda i: (0, 0)),
        ],
        out_shape=[jax.ShapeDtypeStruct((T, LANE), F32), jax.ShapeDtypeStruct((8, LANE), F32)],
        scratch_shapes=[pltpu.VMEM((1, LANE), F32)],
        compiler_params=_cparams(("arbitrary",)),
        name="moe_router",
    )(h2, w_router_pad)


def _row_copy(src_hbm, src_row, dst_vmem, dst_row, sem):
    return pltpu.make_async_copy(src_hbm.at[pl.ds(src_row, 1), :], dst_vmem.at[pl.ds(dst_row, 1), :], sem)


def _expert_kernel(tile_expert_ref, n_tiles_ref, slot_token_ref, dest_ref, h_hbm, w1_ref, w3_ref, w2_ref,
                   y_hbm, rows_ref, hb_ref, acc_ref, stage_ref, gsems, ssems, *, tm, chunk):
    i = pl.program_id(0)
    f = pl.program_id(1)
    nf = pl.num_programs(1)
    n_live = n_tiles_ref[0]
    live = i < n_live
    cur = i % 2
    prev = 1 - cur

    def gather_chunk(tile, part, dst):
        for u in range(chunk):
            row = part * chunk + u
            _row_copy(h_hbm, slot_token_ref[tile * tm + row], rows_ref.at[dst], row, gsems.at[dst]).start()

    def scatter_copy(src, row, dest_row):
        return pltpu.make_async_copy(stage_ref.at[src, pl.ds(row, 1), :], y_hbm.at[pl.ds(dest_row, 1), :],
                                     ssems.at[src])

    def scatter_chunk(part):
        for u in range(chunk):
            row = part * chunk + u
            scatter_copy(prev, row, dest_ref[i * tm + row]).start()

    @pl.when((i == 0) & (f == 0))
    def _():
        stage_ref[1] = jnp.zeros(stage_ref.shape[1:], stage_ref.dtype)

        def issue(part, carry):
            gather_chunk(0, part, 0)
            return carry

        lax.fori_loop(0, nf, issue, 0)

    @pl.when((f == 0) & (i <= n_live))
    def _():
        def wait(r, carry):
            _row_copy(h_hbm, 0, rows_ref.at[cur], r, gsems.at[cur]).wait()
            return carry

        lax.fori_loop(0, tm, wait, 0, unroll=chunk)

    @pl.when((f == 0) & (i >= 1) & (i <= n_live + 1))
    def _():
        def wait(r, carry):
            scatter_copy(cur, r, 0).wait()
            return carry

        lax.fori_loop(0, tm, wait, 0, unroll=chunk)

    @pl.when(live & (f == 0))
    def _():
        hb_ref[...] = rows_ref[cur].astype(BF16)
        acc_ref[...] = jnp.zeros_like(acc_ref)

    @pl.when(live)
    def _():
        gather_chunk(i + 1, f, prev)
        scatter_chunk(f)
        h = hb_ref[...]
        act = (_silu(_dot(h, w1_ref[0])) * _dot(h, w3_ref[0])).astype(BF16)
        acc_ref[...] += _dot(act, w2_ref[0])

    @pl.when(i == n_live)
    def _():
        scatter_chunk(f)

    @pl.when(live & (f == nf - 1))
    def _():
        stage_ref[cur] = acc_ref[...]


def _expert_ffn(h2, tile_expert, n_tiles, slot_token, dest, w1, w3, w2, n_out_rows):
    tm, tf, nf = MOE_TM, MOE_TF, MOE_STEPS
    max_tiles = tile_expert.shape[0]

    def live_tile(i, nt):
        return jnp.minimum(i, nt[0] - 1)

    def f_idx(i, f, nt):
        return jnp.where(i < nt[0], f, nf - 1)

    grid_spec = pltpu.PrefetchScalarGridSpec(
        num_scalar_prefetch=4,
        grid=(max_tiles, nf),
        in_specs=[
            pl.BlockSpec(memory_space=pl.ANY),
            pl.BlockSpec((1, D_MODEL, tf), lambda i, f, te, nt, st, ds: (te[live_tile(i, nt)], 0, f_idx(i, f, nt))),
            pl.BlockSpec((1, D_MODEL, tf), lambda i, f, te, nt, st, ds: (te[live_tile(i, nt)], 0, f_idx(i, f, nt))),
            pl.BlockSpec((1, tf, D_MODEL), lambda i, f, te, nt, st, ds: (te[live_tile(i, nt)], f_idx(i, f, nt), 0)),
        ],
        out_specs=pl.BlockSpec(memory_space=pl.ANY),
        scratch_shapes=[
            pltpu.VMEM((2, tm, D_MODEL), F32),
            pltpu.VMEM((tm, D_MODEL), BF16),
            pltpu.VMEM((tm, D_MODEL), F32),
            pltpu.VMEM((2, tm, D_MODEL), F32),
            pltpu.SemaphoreType.DMA((2,)),
            pltpu.SemaphoreType.DMA((2,)),
        ],
    )
    return pl.pallas_call(
        functools.partial(_expert_kernel, tm=tm, chunk=MOE_CHUNK),
        grid_spec=grid_spec,
        out_shape=jax.ShapeDtypeStruct((n_out_rows, D_MODEL), F32),
        compiler_params=_cparams(("arbitrary", "arbitrary")),
        name="moe_experts",
    )(tile_expert, n_tiles, slot_token, dest, h2, w1, w3, w2)


def _combine_kernel(y1_ref, y2_ref, x_ref, info_ref, mod_ref, g_ref, b_ref, o_ref):
    info = info_ref[...]
    y = info[:, INFO_W:INFO_W + 1] * y1_ref[...] + info[:, INFO_W + 1:INFO_W + 2] * y2_ref[...]
    z = DN_ALPHA * x_ref[...] + mod_ref[0, 5:6, :] * y
    o_ref[...] = _layer_norm(z, g_ref[...], b_ref[...])


def _moe_combine(y_tok, x2, info, mod_l, ln_g, ln_b, seq):
    T = x2.shape[0]
    tm = 512
    per_b = seq // tm
    row = pl.BlockSpec((tm, D_MODEL), lambda i: (i, 0))
    vec = pl.BlockSpec((1, D_MODEL), lambda i: (0, 0))
    return pl.pallas_call(
        _combine_kernel,
        grid=(T // tm,),
        in_specs=[
            row,
            pl.BlockSpec((tm, D_MODEL), lambda i: (T // tm + i, 0)),
            row,
            pl.BlockSpec((tm, LANE), lambda i: (i, 0)),
            pl.BlockSpec((1, N_MOD, D_MODEL), lambda i: (i // per_b, 0, 0)),
            vec, vec,
        ],
        out_specs=row,
        out_shape=jax.ShapeDtypeStruct((T, D_MODEL), F32),
        compiler_params=_cparams(("parallel",)),
        name="moe_combine_ln",
    )(y_tok, y_tok, x2, info, mod_l, ln_g.reshape(1, D_MODEL), ln_b.reshape(1, D_MODEL))


def _moe_sublayer(x2, h2, mod_l, w_router, w1, w3, w2, ln_g, ln_b, seq):
    T = x2.shape[0]
    tm = MOE_TM
    wr_pad = jnp.zeros((D_MODEL, LANE), BF16).at[:, :N_EXPERTS].set(w_router.astype(BF16))
    info, cnt = _router(h2, wr_pad)
    experts = info[:, INFO_I:INFO_I + TOP_K].astype(jnp.int32)
    ranks = info[:, INFO_R:INFO_R + TOP_K].astype(jnp.int32)
    counts = cnt[0, :N_EXPERTS].astype(jnp.int32)
    tiles_e = (counts + tm - 1) // tm
    tile_end = jnp.cumsum(tiles_e)
    tile_start = tile_end - tiles_e
    max_tiles = (T * TOP_K + N_EXPERTS * (tm - 1)) // tm + 2
    slots = (tile_start * tm)[experts] + ranks
    tile_ids = jnp.arange(max_tiles, dtype=jnp.int32)
    tile_expert = jnp.minimum(jnp.sum(tile_ids[:, None] >= tile_end[None, :], axis=1), N_EXPERTS - 1).astype(jnp.int32)
    n_tiles = tile_end[-1:].astype(jnp.int32)
    token_ids = jnp.broadcast_to(jnp.arange(T, dtype=jnp.int32)[:, None], (T, TOP_K))
    n_slots = max_tiles * tm
    spare = TOP_K * T + jnp.arange(n_slots, dtype=jnp.int32) % tm
    out_rows = jnp.arange(TOP_K, dtype=jnp.int32)[None, :] * T + token_ids
    slot_dest = spare.at[slots.reshape(-1)].set(out_rows.reshape(-1))
    slot_token = slot_dest % T
    dest = jnp.concatenate([spare[:tm], slot_dest[:n_slots - tm]])
    y_tok = _expert_ffn(h2, tile_expert, n_tiles, slot_token, dest, w1, w3, w2, TOP_K * T + tm)
    return _moe_combine(y_tok, x2, info, mod_l, ln_g, ln_b, seq)


def _pack_w_in(w_in_l):
    low = jnp.zeros((D_MODEL, N_PROJ - U_LOW * LANE), w_in_l.dtype).at[:, :GLA_RANK].set(w_in_l[:, _REF_LOW0:_REF_G0])
    a_cols = [w_in_l[:, part * A_WIDTH + g * W4:part * A_WIDTH + (g + 1) * W4]
              for g in range(len(DIL_GROUPS)) for part in range(3)]
    return jnp.concatenate(a_cols + [w_in_l[:, 3 * A_WIDTH:_REF_LOW0], w_in_l[:, _REF_G0:], low], axis=1).astype(BF16)


def _pack_w_up(w_up_l):
    pad = jnp.zeros((LANE, GLA_DK), F32).at[:GLA_RANK].set(w_up_l)
    hi = pad.astype(BF16)
    lo = (pad - hi.astype(F32)).astype(BF16)
    return hi, lo


def kernel(x, c, w_ada, b_ada, ln_g, ln_b, w_in, w_alpha_up, b_alpha, gla_norm_g, b_merge, w_branch_a,
           w_branch_b, w_out, ffn_w1, ffn_w3, ffn_w2, w_router, moe_w1, moe_w3, moe_w2):
    batch, seq, _ = x.shape
    T = batch * seq
    x2 = x.reshape(T, D_MODEL)
    mod = _adaln_mod(c, w_ada, b_ada)
    cast_block = {"w1": (1, D_MODEL, MXU_DIM), "w3": (1, D_MODEL, MXU_DIM), "w2": (1, MXU_DIM, D_MODEL)}
    to_cast = [(l, name, w[l // 2]) for l in range(DEPTH) if l % 2 == 1
               for name, w in (("w1", moe_w1), ("w2", moe_w2), ("w3", moe_w3))]
    moe_bf16 = {}

    def next_side(before_layer):
        if to_cast and to_cast[0][0] >= before_layer:
            layer, name, w = to_cast.pop(0)
            return (layer, name), (w, cast_block[name])
        return None, None

    for l in range(DEPTH):
        mod_l = mod[l]
        key, side = next_side(l)
        proj, cast = _in_projection(x2, mod_l, _pack_w_in(w_in[l]), seq, side)
        if key is not None:
            moe_bf16[key] = cast
        groups = [_dilated_group(proj, batch, seq, g) for g in range(len(DIL_GROUPS))]
        w_up_hi, w_up_lo = _pack_w_up(w_alpha_up[l])
        y_b = _gla_branch(proj, batch, seq, w_up_hi, w_up_lo, b_alpha[l], gla_norm_g[l])
        mixed = _branch_mix([g[0] for g in groups], [g[1] for g in groups], y_b, proj, b_merge[l],
                            w_branch_a[l].astype(BF16), w_branch_b[l].astype(BF16))
        moe_layer = l % 2 == 1
        x2, h2 = _out_projection(mixed, x2, mod_l, w_out[l].astype(BF16), ln_g[l, 0], ln_b[l, 0], seq, moe_layer)
        e = l // 2
        if moe_layer:
            to_cast = [item for item in to_cast if item[0] != l]
            w1, w3, w2 = (moe_bf16[(l, name)] if (l, name) in moe_bf16 else w[e].astype(BF16)
                          for name, w in (("w1", moe_w1), ("w3", moe_w3), ("w2", moe_w2)))
            x2 = _moe_sublayer(x2, h2, mod_l, w_router[e], w1, w3, w2, ln_g[l, 1], ln_b[l, 1], seq)
        else:
            key, side = next_side(l + 1)
            x2, cast = _dense_ffn(x2, mod_l, ffn_w1[e].astype(BF16), ffn_w3[e].astype(BF16), ffn_w2[e].astype(BF16),
                                  ln_g[l, 1], ln_b[l, 1], seq, side)
            if key is not None:
                moe_bf16[key] = cast
    return x2.reshape(batch, seq, D_MODEL)
```

```python
import functools

import jax
import jax.numpy as jnp
from jax import lax
from jax.experimental import pallas as pl
from jax.experimental.pallas import tpu as pltpu

F32 = jnp.float32
BF16 = jnp.bfloat16

D_MODEL = 2048
DEPTH = 2
HEAD_DIM = 128
DIL_GROUPS = ((128, 1), (512, 4), (2048, 16))
HEADS_PER_GROUP = 4
A_HEADS = HEADS_PER_GROUP * len(DIL_GROUPS)
A_WIDTH = A_HEADS * HEAD_DIM
A_OUT = HEADS_PER_GROUP * HEAD_DIM
ALIBI_MAX_EXP = 8.0
GLA_HEADS = 4
GLA_DK = D_MODEL // 2
GLA_DV = D_MODEL
GLA_HK = GLA_DK // GLA_HEADS
GLA_HV = GLA_DV // GLA_HEADS
GLA_RANK = 16
GLA_TAU = 16.0
GLA_CHUNK = 64
N_BRANCH = 2
FFN_DENSE = 5632
N_EXPERTS = 8
TOP_K = 2
FFN_EXPERT = 7168
N_MOD = 6
LN_EPS = 1e-5
NORM_EPS = 1e-6
DN_ALPHA = (2 * DEPTH) ** 0.25

LANE = 128
MXU_DIM = 256
VMEM_LIMIT_MB = 56

U_GROUP = 12
U_A_END = 36
U_BQ, U_BK, U_BV, U_BR = 36, 44, 52, 68
U_GA, U_GB = 84, 100
U_LOW = 116
N_UNITS = 120
N_PROJ = N_UNITS * LANE
BAND = 128
W4 = HEADS_PER_GROUP * HEAD_DIM
PERM_TILE = 1024

_REF_LOW0 = 3 * A_WIDTH + 2 * GLA_DK + 2 * GLA_DV
_REF_G0 = _REF_LOW0 + GLA_RANK


def _cparams(semantics, vmem_mb=VMEM_LIMIT_MB):
    return pltpu.CompilerParams(dimension_semantics=semantics, vmem_limit_bytes=vmem_mb << 20)


def _dot(a, b):
    return jnp.dot(a, b, preferred_element_type=F32)


def _dot_nt(a, b):
    return lax.dot_general(a, b, (((1,), (1,)), ((), ())), preferred_element_type=F32)


def _dot_tn(a, b):
    return lax.dot_general(a, b, (((0,), (0,)), ((), ())), preferred_element_type=F32)


def _sigmoid(x):
    return 1.0 / (1.0 + jnp.exp(-x))


def _silu(x):
    return x * _sigmoid(x)


def _layer_norm(z, g, b):
    mu = jnp.mean(z, axis=-1, keepdims=True)
    zc = z - mu
    var = jnp.mean(zc * zc, axis=-1, keepdims=True)
    return zc * lax.rsqrt(var + LN_EPS) * g + b


def _mod_kernel(c_ref, w_ref, b_ref, o_ref):
    s = _silu(c_ref[...]).astype(BF16)
    o_ref[0] = _dot(s, w_ref[0].astype(BF16)) + b_ref[0]


def _adaln_mod(c, w_ada, b_ada):
    B = c.shape[0]
    rows = 8
    tn = 1024
    c_pad = jnp.zeros((rows, D_MODEL), F32).at[:B].set(c)
    n_out = N_MOD * D_MODEL
    out = pl.pallas_call(
        _mod_kernel,
        grid=(DEPTH, n_out // tn),
        in_specs=[
            pl.BlockSpec((rows, D_MODEL), lambda l, j: (0, 0)),
            pl.BlockSpec((1, D_MODEL, tn), lambda l, j: (l, 0, j)),
            pl.BlockSpec((1, 1, tn), lambda l, j: (l, 0, j)),
        ],
        out_specs=pl.BlockSpec((1, rows, tn), lambda l, j: (l, 0, j)),
        out_shape=jax.ShapeDtypeStruct((DEPTH, rows, n_out), F32),
        compiler_params=_cparams(("arbitrary", "arbitrary")),
        name="adaln_mod",
    )(c_pad, w_ada, b_ada.reshape(DEPTH, 1, n_out))
    return out[:, :B].reshape(DEPTH, B, N_MOD, D_MODEL)


def _unit_dilation(unit):
    return DIL_GROUPS[unit // U_GROUP][1] if unit < U_A_END else 1


def _side_cast_specs(src, block, steps_inner):
    n_e, n_r, n_c = src.shape
    _, br, bc = block
    per_e = (n_r // br) * (n_c // bc)
    last = n_e * per_e - 1

    def index(i, j):
        s = jnp.minimum(i * steps_inner + j, last)
        rem = s % per_e
        return s // per_e, rem // (n_c // bc), rem % (n_c // bc)

    return pl.BlockSpec(block, index), last + 1


def _inproj_kernel(*refs, tm, tn, side):
    if side:
        x_ref, mod_ref, w_ref, side_ref, o_ref, side_out_ref, h_ref, slab_ref = refs
        side_out_ref[...] = side_ref[...].astype(side_out_ref.dtype)
    else:
        x_ref, mod_ref, w_ref, o_ref, h_ref, slab_ref = refs
    j = pl.program_id(1)

    @pl.when(j == 0)
    def _():
        shift = mod_ref[0, 0:1, :]
        scale = mod_ref[0, 1:2, :]
        h_ref[...] = (x_ref[...] * (1.0 + scale) + shift).astype(BF16)

    acc = _dot(h_ref[...], w_ref[...])
    parts = tn // W4
    tile_dils = [[_unit_dilation((jj * tn + p * W4) // LANE) for p in range(parts)] for jj in range(N_PROJ // tn)]
    perm_tiles = [jj for jj, ds in enumerate(tile_dils) if any(d > 1 for d in ds)]
    is_perm = functools.reduce(jnp.logical_or, [j == jj for jj in perm_tiles])

    @pl.when(jnp.logical_not(is_perm))
    def _():
        o_ref[...] = acc.astype(o_ref.dtype)

    for jj in perm_tiles:
        @pl.when(j == jj)
        def _(jj=jj):
            for p, dil in enumerate(tile_dils[jj]):
                c0 = p * W4
                if dil == 1:
                    o_ref[:, c0:c0 + W4] = acc[:, c0:c0 + W4].astype(o_ref.dtype)
                    continue
                n = tm // dil
                for s in range(W4 // LANE):
                    slab_ref[s] = acc[:, c0 + s * LANE:c0 + (s + 1) * LANE]
                for s in range(W4 // LANE):
                    for r in range(dil):
                        o_ref[r * n:(r + 1) * n, c0 + s * LANE:c0 + (s + 1) * LANE] = (
                            slab_ref[s, pl.ds(r, n, stride=dil), :].astype(o_ref.dtype))


def _in_projection(x2, mod_l, w_cat, seq, side=None):
    T = x2.shape[0]
    tm = PERM_TILE
    tn = 1024
    per_b = seq // tm
    grid = (T // tm, N_PROJ // tn)
    in_specs = [
        pl.BlockSpec((tm, D_MODEL), lambda i, j: (i, 0)),
        pl.BlockSpec((1, N_MOD, D_MODEL), lambda i, j: (i // per_b, 0, 0)),
        pl.BlockSpec((D_MODEL, tn), lambda i, j: (0, j)),
    ]
    out_specs = [pl.BlockSpec((tm, tn), lambda i, j: (i, j))]
    out_shape = [jax.ShapeDtypeStruct((T, N_PROJ), BF16)]
    args = [x2, mod_l, w_cat]
    if side is not None:
        src, block = side
        spec, n_blocks = _side_cast_specs(src, block, grid[1])
        assert n_blocks <= grid[0] * grid[1]
        in_specs.append(spec)
        out_specs.append(spec)
        out_shape.append(jax.ShapeDtypeStruct(src.shape, BF16))
        args.append(src)
    outs = pl.pallas_call(
        functools.partial(_inproj_kernel, tm=tm, tn=tn, side=side is not None),
        grid=grid,
        in_specs=in_specs,
        out_specs=out_specs,
        out_shape=out_shape,
        scratch_shapes=[pltpu.VMEM((tm, D_MODEL), BF16), pltpu.VMEM((W4 // LANE, tm, LANE), F32)],
        compiler_params=_cparams(("arbitrary", "arbitrary")),
        name="in_projection",
    )(*args)
    return (outs[0], outs[1]) if side is not None else (outs[0], None)


LSE_LANES = LANE // HEADS_PER_GROUP


def _attn_kernel(*refs, c, nt, bias_scale, has_prev):
    if has_prev:
        q_ref, kp_ref, k_ref, vp_ref, v_ref, o_ref, lse_ref = refs
        first_run = pl.program_id(1) == 0
    else:
        q_ref, k_ref, v_ref, o_ref, lse_ref = refs
    qi = lax.broadcasted_iota(jnp.int32, (BAND, BAND), 0)
    ki = lax.broadcasted_iota(jnp.int32, (BAND, BAND), 1)
    valid_prev = ki >= qi
    valid_cur = ki <= qi
    delta_prev = (qi + BAND - ki).astype(F32)
    delta_cur = (qi - ki).astype(F32)
    neg_inf = jnp.float32(-jnp.inf)
    sm_scale = HEAD_DIM ** -0.5

    def rows(sb):
        if c >= BAND:
            per = c // BAND
            return sb // per, slice((sb % per) * BAND, (sb % per + 1) * BAND)
        per = BAND // c
        return slice(sb * per, (sb + 1) * per), slice(None)

    def load(ref, sb, cs):
        t, r = rows(sb)
        return ref[t, r, cs].reshape(BAND, cs.stop - cs.start)

    def store(ref, sb, cs, val):
        t, r = rows(sb)
        if c < BAND:
            val = val.reshape(BAND // c, c, cs.stop - cs.start)
        ref[t, r, cs] = val

    heads = range(HEADS_PER_GROUP)
    cols = [slice(hh * HEAD_DIM, (hh + 1) * HEAD_DIM) for hh in heads]
    bias_cur = jnp.concatenate([jnp.where(valid_cur, -bias_scale[hh] * delta_cur, neg_inf) for hh in heads], axis=0)
    bias_prev = jnp.concatenate([jnp.where(valid_prev, -bias_scale[hh] * delta_prev, neg_inf) for hh in heads], axis=0)
    head_rows = [slice(hh * BAND, (hh + 1) * BAND) for hh in heads]
    for sb in range(nt * c // BAND):
        with_prev = sb > 0 or has_prev
        qs = [load(q_ref, sb, cs) for cs in cols]
        v_cur = [load(v_ref, sb, cs) for cs in cols]
        s_cur = jnp.concatenate([_dot_nt(qs[hh], load(k_ref, sb, cols[hh])) for hh in heads], axis=0)
        s_cur = s_cur * sm_scale + bias_cur
        m = jnp.max(s_cur, axis=1, keepdims=True)
        if with_prev:
            if sb > 0:
                k_prev = [load(k_ref, sb - 1, cs) for cs in cols]
                v_prev = [load(v_ref, sb - 1, cs) for cs in cols]
            else:
                k_prev = [kp_ref[0, :, cs] for cs in cols]
                v_prev = [vp_ref[0, :, cs] for cs in cols]
            s_prev = jnp.concatenate([_dot_nt(qs[hh], k_prev[hh]) for hh in heads], axis=0)
            s_prev = s_prev * sm_scale + bias_prev
            if sb == 0:
                s_prev = jnp.where(first_run, neg_inf, s_prev)
            m = jnp.maximum(m, jnp.max(s_prev, axis=1, keepdims=True))
            p_prev = jnp.exp(s_prev - m)
        p_cur = jnp.exp(s_cur - m)
        den = jnp.sum(p_cur, axis=1, keepdims=True)
        if with_prev:
            den = den + jnp.sum(p_prev, axis=1, keepdims=True)
        inv_den = 1.0 / den
        lse = m + jnp.log(den)
        p_cur = p_cur.astype(BF16)
        if with_prev:
            p_prev = p_prev.astype(BF16)
        for hh in heads:
            o = _dot(p_cur[head_rows[hh]], v_cur[hh])
            if with_prev:
                o = o + _dot(p_prev[head_rows[hh]], v_prev[hh])
            store(o_ref, sb, cols[hh], o * inv_den[head_rows[hh]])
            store(lse_ref, sb, slice(hh * LSE_LANES, (hh + 1) * LSE_LANES),
                  jnp.broadcast_to(lse[head_rows[hh]], (BAND, LSE_LANES)))


def _dilated_group(proj, batch, seq, g):
    _, dil = DIL_GROUPS[g]
    T = batch * seq
    tiles = T // PERM_TILE
    per_b = seq // PERM_TILE
    qcol, kcol, vcol = (g * U_GROUP * LANE // W4 + i for i in range(3))
    slopes = [2.0 ** (-ALIBI_MAX_EXP * (g * HEADS_PER_GROUP + h + 1) / A_HEADS) * dil
              for h in range(HEADS_PER_GROUP)]
    if dil == 1:
        c, nt = PERM_TILE, 1
        runs = proj.reshape(tiles, PERM_TILE, N_PROJ)
        bands = proj.reshape(T // BAND, BAND, N_PROJ)
        per_run = PERM_TILE // BAND
        cur = lambda col: pl.BlockSpec((1, c, W4), lambda b, n: (b * per_b + n, 0, col))
        prev = lambda col: pl.BlockSpec(
            (1, BAND, W4), lambda b, n: (jnp.maximum((b * per_b + n) * per_run - 1, 0), 0, col))
        grid = (batch, per_b)
        in_specs = [cur(qcol), prev(kcol), cur(kcol), prev(vcol), cur(vcol)]
        args = (runs, bands, runs, bands, runs)
        out_specs = [pl.BlockSpec((1, c, W4), lambda b, n: (b * per_b + n, 0, 0)),
                     pl.BlockSpec((1, c, LANE), lambda b, n: (b * per_b + n, 0, 0))]
        out_shape = [jax.ShapeDtypeStruct((tiles, c, W4), F32), jax.ShapeDtypeStruct((tiles, c, LANE), F32)]
        semantics = ("parallel", "arbitrary")
    else:
        c, nt = PERM_TILE // dil, per_b
        runs = proj.reshape(tiles, dil, c, N_PROJ)
        cur = lambda col: pl.BlockSpec((nt, None, c, W4), lambda b, r: (b, r, 0, col))
        grid = (batch, dil)
        in_specs = [cur(qcol), cur(kcol), cur(vcol)]
        args = (runs, runs, runs)
        out_specs = [pl.BlockSpec((nt, None, c, W4), lambda b, r: (b, r, 0, 0)),
                     pl.BlockSpec((nt, None, c, LANE), lambda b, r: (b, r, 0, 0))]
        out_shape = [jax.ShapeDtypeStruct((tiles, dil, c, W4), F32),
                     jax.ShapeDtypeStruct((tiles, dil, c, LANE), F32)]
        semantics = ("parallel", "parallel")
    o, lse = pl.pallas_call(
        functools.partial(_attn_kernel, c=c, nt=nt, bias_scale=slopes, has_prev=dil == 1),
        grid=grid,
        in_specs=in_specs,
        out_specs=out_specs,
        out_shape=out_shape,
        compiler_params=_cparams(semantics),
        name=f"dilated_attn_g{g}",
    )(*args)
    return o.reshape(T, W4), lse.reshape(T, LANE)


GLA_GROUP = 256


def _gla_kernel(q_ref, k_ref, v_ref, r_ref, low_ref, whi_ref, wlo_ref, balpha_ref, gnorm_ref,
                y_ref, state_ref, *, tb):
    @pl.when(pl.program_id(2) == 0)
    def _():
        state_ref[...] = jnp.zeros_like(state_ref)

    C = GLA_CHUNK
    G = GLA_GROUP
    low = low_ref[...]
    z = _dot(low, whi_ref[...]) + _dot(low, wlo_ref[...]) + balpha_ref[...]
    log_a = (jnp.minimum(z, 0.0) - jnp.log(1.0 + jnp.exp(-jnp.abs(z)))) * (1.0 / GLA_TAU)

    ri = lax.broadcasted_iota(jnp.int32, (G, G), 0)
    ci = lax.broadcasted_iota(jnp.int32, (G, G), 1)
    same_chunk = (ri // C) == (ci // C)
    causal = same_chunk & (ci <= ri)
    tril = causal.astype(BF16)
    sel_r = lax.broadcasted_iota(jnp.int32, (G, (G // C) * LANE), 0)
    sel_c = lax.broadcasted_iota(jnp.int32, (G, (G // C) * LANE), 1)
    chunk_sel = ((sel_r // C) == (sel_c // LANE)).astype(BF16)

    for grp in range(tb // G):
        rows = slice(grp * G, (grp + 1) * G)
        la = log_a[rows]
        la_hi = la.astype(BF16)
        la_lo = (la - la_hi.astype(F32)).astype(BF16)
        gc = _dot(tril, la_hi) + _dot(tril, la_lo)
        gtot_cols = _dot_tn(la_hi, chunk_sel) + _dot_tn(la_lo, chunk_sel)
        q_g = q_ref[rows, :].astype(F32) * (GLA_HK ** -0.5)
        k_g = k_ref[rows, :].astype(F32)
        chunks = [slice(c * C, (c + 1) * C) for c in range(G // C)]
        g_mid = jnp.concatenate([jnp.broadcast_to(gc[cr][C // 2 - 1:C // 2], (C, GLA_HK)) for cr in chunks], axis=0)
        g_end = jnp.concatenate([jnp.broadcast_to(gc[cr][C - 1:C], (C, GLA_HK)) for cr in chunks], axis=0)
        qe = (q_g * jnp.exp(gc - g_mid)).astype(BF16)
        ke = (k_g * jnp.exp(g_mid - gc)).astype(BF16)
        q_in = (q_g * jnp.exp(gc)).astype(BF16)
        k_out = (k_g * jnp.exp(g_end - gc)).astype(BF16)
        decay_cols = jnp.exp(gtot_cols)
        updates = [_dot_tn(k_out[cr], v_ref[grp * G + c * C:grp * G + (c + 1) * C, :]) for c, cr in enumerate(chunks)]
        state = state_ref[...]
        o_inter = []
        for c, cr in enumerate(chunks):
            o_inter.append(_dot(q_in[cr], state.astype(BF16)))
            decay = decay_cols[:, c * LANE:(c + 1) * LANE]
            state = jnp.concatenate(
                [state[:, j * LANE:(j + 1) * LANE] * decay for j in range(GLA_HV // LANE)], axis=1) + updates[c]
        state_ref[...] = state
        att = jnp.where(causal, _dot_nt(qe, ke), 0.0).astype(BF16)
        o = _dot(att, v_ref[rows, :]) + jnp.concatenate(o_inter, axis=0)
        ms = jnp.mean(o * o, axis=-1, keepdims=True)
        o = o * lax.rsqrt(ms + NORM_EPS) * gnorm_ref[...]
        y_ref[rows, :] = (_silu(r_ref[rows, :].astype(F32)) * o).astype(y_ref.dtype)


def _gla_branch(proj, batch, seq, w_up_hi, w_up_lo, b_alpha, gnorm):
    T = batch * seq
    tb = 512
    per_b = seq // tb

    def rowcol(unit, width):
        return lambda b, h, n: (b * per_b + n, unit * LANE // width + h)

    return pl.pallas_call(
        functools.partial(_gla_kernel, tb=tb),
        grid=(batch, GLA_HEADS, per_b),
        in_specs=[
            pl.BlockSpec((tb, GLA_HK), rowcol(U_BQ, GLA_HK)),
            pl.BlockSpec((tb, GLA_HK), rowcol(U_BK, GLA_HK)),
            pl.BlockSpec((tb, GLA_HV), rowcol(U_BV, GLA_HV)),
            pl.BlockSpec((tb, GLA_HV), rowcol(U_BR, GLA_HV)),
            pl.BlockSpec((tb, LANE), lambda b, h, n: (b * per_b + n, U_LOW)),
            pl.BlockSpec((LANE, GLA_HK), lambda b, h, n: (0, h)),
            pl.BlockSpec((LANE, GLA_HK), lambda b, h, n: (0, h)),
            pl.BlockSpec((1, GLA_HK), lambda b, h, n: (0, h)),
            pl.BlockSpec((1, GLA_HV), lambda b, h, n: (0, 0)),
        ],
        out_specs=pl.BlockSpec((tb, GLA_HV), lambda b, h, n: (b * per_b + n, h)),
        out_shape=jax.ShapeDtypeStruct((T, GLA_DV), BF16),
        scratch_shapes=[
            pltpu.VMEM((GLA_HK, GLA_HV), F32),
        ],
        compiler_params=_cparams(("parallel", "parallel", "arbitrary")),
        name="gla_branch",
    )(proj, proj, proj, proj, proj, w_up_hi, w_up_lo, b_alpha.reshape(1, GLA_DK), gnorm.reshape(1, GLA_HV))


def _branch_kernel(o0_ref, o1_ref, o2_ref, l0_ref, l1_ref, l2_ref, yb_ref, ga_ref, gb_ref, bm_ref,
                   wa_ref, wb_ref, out_ref, ya_ref, oslab_ref, lslab_ref, *, tm):
    @pl.when(pl.program_id(1) == 0)
    def _():
        for gi, (o_ref, l_ref) in enumerate(((o1_ref, l1_ref), (o2_ref, l2_ref))):
            dil = DIL_GROUPS[gi + 1][1]
            n = tm // dil
            for r in range(dil):
                src = slice(r * n, (r + 1) * n)
                dst = pl.ds(r, n, stride=dil)
                for h in range(HEADS_PER_GROUP):
                    oslab_ref[gi, h, dst, :] = o_ref[src, h * HEAD_DIM:(h + 1) * HEAD_DIM]
                lslab_ref[gi, dst, :] = l_ref[src, :]
        l0, l1, l2 = l0_ref[...], lslab_ref[0], lslab_ref[1]
        m = jnp.maximum(jnp.maximum(l0, l1), l2)
        e0, e1, e2 = jnp.exp(l0 - m), jnp.exp(l1 - m), jnp.exp(l2 - m)
        inv = 1.0 / (e0 + e1 + e2)
        w0, w1, w2 = e0 * inv, e1 * inv, e2 * inv
        for h in range(HEADS_PER_GROUP):
            cs = slice(h * HEAD_DIM, (h + 1) * HEAD_DIM)
            lane = slice(h * LSE_LANES, h * LSE_LANES + 1)
            ya = o0_ref[:, cs] * w0[:, lane] + oslab_ref[0, h] * w1[:, lane] + oslab_ref[1, h] * w2[:, lane]
            ya_ref[:, cs] = ya.astype(BF16)

    pa = _dot(ya_ref[...], wa_ref[...])
    pb = _dot(yb_ref[...], wb_ref[...])
    gate_a = _sigmoid(ga_ref[...].astype(F32) + bm_ref[0:1, :])
    gate_b = _sigmoid(gb_ref[...].astype(F32) + bm_ref[1:2, :])
    out_ref[...] = (gate_a * pa + gate_b * pb).astype(out_ref.dtype)


def _branch_mix(outs, lses, y_b, proj, b_merge, wa, wb):
    T = y_b.shape[0]
    tm, tn = PERM_TILE, 512
    row = lambda i, j: (i, 0)
    n_dilated = len(DIL_GROUPS) - 1
    return pl.pallas_call(
        functools.partial(_branch_kernel, tm=tm),
        grid=(T // tm, D_MODEL // tn),
        in_specs=[pl.BlockSpec((tm, A_OUT), row)] * 3 + [pl.BlockSpec((tm, LANE), row)] * 3 + [
            pl.BlockSpec((tm, GLA_DV), row),
            pl.BlockSpec((tm, tn), lambda i, j: (i, U_GA * LANE // tn + j)),
            pl.BlockSpec((tm, tn), lambda i, j: (i, U_GB * LANE // tn + j)),
            pl.BlockSpec((N_BRANCH, tn), lambda i, j: (0, j)),
            pl.BlockSpec((A_OUT, tn), lambda i, j: (0, j)),
            pl.BlockSpec((GLA_DV, tn), lambda i, j: (0, j)),
        ],
        out_specs=pl.BlockSpec((tm, tn), lambda i, j: (i, j)),
        out_shape=jax.ShapeDtypeStruct((T, D_MODEL), BF16),
        scratch_shapes=[
            pltpu.VMEM((tm, A_OUT), BF16),
            pltpu.VMEM((n_dilated, HEADS_PER_GROUP, tm, HEAD_DIM), F32),
            pltpu.VMEM((n_dilated, tm, LANE), F32),
        ],
        compiler_params=_cparams(("parallel", "arbitrary")),
        name="branch_mix",
    )(*outs, *lses, y_b, proj, proj, b_merge, wa, wb)


def _outproj_kernel(mixed_ref, x_ref, mod_ref, w_ref, g_ref, b_ref, *out_refs, emit_h, tm, parts):
    gate = mod_ref[0, 2:3, :]
    for p in range(parts):
        rs = slice(p * (tm // parts), (p + 1) * (tm // parts))
        y = _dot(mixed_ref[rs, :], w_ref[...])
        xn = _layer_norm(DN_ALPHA * x_ref[rs, :] + gate * y, g_ref[...], b_ref[...])
        out_refs[0][rs, :] = xn
        if emit_h:
            out_refs[1][rs, :] = xn * (1.0 + mod_ref[0, 4:5, :]) + mod_ref[0, 3:4, :]


def _out_projection(mixed, x2, mod_l, w_out, ln_g, ln_b, seq, emit_h):
    T = x2.shape[0]
    tm, parts = 512, 2
    per_b = seq // tm
    row = pl.BlockSpec((tm, D_MODEL), lambda i: (i, 0))
    vec = pl.BlockSpec((1, D_MODEL), lambda i: (0, 0))
    n_out = 2 if emit_h else 1
    outs = pl.pallas_call(
        functools.partial(_outproj_kernel, emit_h=emit_h, tm=tm, parts=parts),
        grid=(T // tm,),
        in_specs=[
            row, row,
            pl.BlockSpec((1, N_MOD, D_MODEL), lambda i: (i // per_b, 0, 0)),
            pl.BlockSpec((D_MODEL, D_MODEL), lambda i: (0, 0), pipeline_mode=pl.Buffered(1)),
            vec, vec,
        ],
        out_specs=[row] * n_out,
        out_shape=[jax.ShapeDtypeStruct((T, D_MODEL), F32)] * n_out,
        compiler_params=_cparams(("parallel",)),
        name="out_projection_ln",
    )(mixed, x2, mod_l, w_out, ln_g.reshape(1, D_MODEL), ln_b.reshape(1, D_MODEL))
    return outs if emit_h else (outs[0], None)


def _ffn_kernel(*refs, side):
    if side:
        x_ref, mod_ref, w1_ref, w3_ref, w2_ref, g_ref, b_ref, side_ref, o_ref, side_out_ref, h_ref, acc_ref = refs
        side_out_ref[...] = side_ref[...].astype(side_out_ref.dtype)
    else:
        x_ref, mod_ref, w1_ref, w3_ref, w2_ref, g_ref, b_ref, o_ref, h_ref, acc_ref = refs
    f = pl.program_id(1)

    @pl.when(f == 0)
    def _():
        h_ref[...] = (x_ref[...] * (1.0 + mod_ref[0, 4:5, :]) + mod_ref[0, 3:4, :]).astype(BF16)
        acc_ref[...] = jnp.zeros_like(acc_ref)

    h = h_ref[...]
    act = (_silu(_dot(h, w1_ref[...])) * _dot(h, w3_ref[...])).astype(BF16)
    acc_ref[...] += _dot(act, w2_ref[...])

    @pl.when(f == pl.num_programs(1) - 1)
    def _():
        z = DN_ALPHA * x_ref[...] + mod_ref[0, 5:6, :] * acc_ref[...]
        o_ref[...] = _layer_norm(z, g_ref[...], b_ref[...])


def _dense_ffn(x2, mod_l, w1, w3, w2, ln_g, ln_b, seq, side=None):
    T = x2.shape[0]
    tm, tf = 512, 512
    per_b = seq // tm
    grid = (T // tm, FFN_DENSE // tf)
    row = pl.BlockSpec((tm, D_MODEL), lambda i, f: (i, 0))
    vec = pl.BlockSpec((1, D_MODEL), lambda i, f: (0, 0))
    in_specs = [
        row,
        pl.BlockSpec((1, N_MOD, D_MODEL), lambda i, f: (i // per_b, 0, 0)),
        pl.BlockSpec((D_MODEL, tf), lambda i, f: (0, f)),
        pl.BlockSpec((D_MODEL, tf), lambda i, f: (0, f)),
        pl.BlockSpec((tf, D_MODEL), lambda i, f: (f, 0)),
        vec, vec,
    ]
    out_specs = [row]
    out_shape = [jax.ShapeDtypeStruct((T, D_MODEL), F32)]
    args = [x2, mod_l, w1, w3, w2, ln_g.reshape(1, D_MODEL), ln_b.reshape(1, D_MODEL)]
    if side is not None:
        src, block = side
        spec, n_blocks = _side_cast_specs(src, block, grid[1])
        assert n_blocks <= grid[0] * grid[1]
        in_specs.append(spec)
        out_specs.append(spec)
        out_shape.append(jax.ShapeDtypeStruct(src.shape, BF16))
        args.append(src)
    outs = pl.pallas_call(
        functools.partial(_ffn_kernel, side=side is not None),
        grid=grid,
        in_specs=in_specs,
        out_specs=out_specs,
        out_shape=out_shape,
        scratch_shapes=[pltpu.VMEM((tm, D_MODEL), BF16), pltpu.VMEM((tm, D_MODEL), F32)],
        compiler_params=_cparams(("arbitrary", "arbitrary")),
        name="dense_ffn_ln",
    )(*args)
    return (outs[0], outs[1]) if side is not None else (outs[0], None)


MOE_TF = 1024
MOE_STEPS = FFN_EXPERT // MOE_TF
MOE_CHUNK = 80
MOE_TM = MOE_STEPS * MOE_CHUNK
INFO_I, INFO_W, INFO_R = 0, 2, 4


def _router_kernel(h_ref, wr_ref, info_ref, cnt_ref, carry_ref, *, tm):
    @pl.when(pl.program_id(0) == 0)
    def _():
        carry_ref[...] = jnp.zeros_like(carry_ref)

    logits = _dot(h_ref[...].astype(BF16), wr_ref[...])
    lane = lax.broadcasted_iota(jnp.int32, (tm, LANE), 1).astype(F32)
    neg_inf = jnp.float32(-jnp.inf)
    lg = jnp.where(lane < N_EXPERTS, logits, neg_inf)
    m1 = jnp.max(lg, axis=1, keepdims=True)
    i1 = jnp.min(jnp.where(lg == m1, lane, float(LANE)), axis=1, keepdims=True)
    lg2 = jnp.where(lane == i1, neg_inf, lg)
    m2 = jnp.max(lg2, axis=1, keepdims=True)
    i2 = jnp.min(jnp.where(lg2 == m2, lane, float(LANE)), axis=1, keepdims=True)
    e = jnp.exp(m2 - m1)
    w1 = 1.0 / (1.0 + e)
    w2 = e / (1.0 + e)
    oh1 = lane == i1
    oh2 = lane == i2
    onehot = (oh1 | oh2).astype(BF16)
    ri = lax.broadcasted_iota(jnp.int32, (tm, tm), 0)
    ci = lax.broadcasted_iota(jnp.int32, (tm, tm), 1)
    strict_lower = (ci < ri).astype(BF16)
    rank = _dot(strict_lower, onehot) + carry_ref[...]
    r1 = jnp.sum(jnp.where(oh1, rank, 0.0), axis=1, keepdims=True)
    r2 = jnp.sum(jnp.where(oh2, rank, 0.0), axis=1, keepdims=True)
    carry = carry_ref[...] + jnp.sum(onehot.astype(F32), axis=0, keepdims=True)
    carry_ref[...] = carry
    cnt_ref[...] = jnp.broadcast_to(carry, cnt_ref.shape)
    info = jnp.where(lane == INFO_I, i1, 0.0)
    info = jnp.where(lane == INFO_I + 1, i2, info)
    info = jnp.where(lane == INFO_W, w1, info)
    info = jnp.where(lane == INFO_W + 1, w2, info)
    info = jnp.where(lane == INFO_R, r1, info)
    info = jnp.where(lane == INFO_R + 1, r2, info)
    info_ref[...] = info


def _router(h2, w_router_pad):
    T = h2.shape[0]
    tm = 512
    return pl.pallas_call(
        functools.partial(_router_kernel, tm=tm),
        grid=(T // tm,),
        in_specs=[
            pl.BlockSpec((tm, D_MODEL), lambda i: (i, 0)),
            pl.BlockSpec((D_MODEL, LANE), lambda i: (0, 0)),
        ],
        out_specs=[
            pl.BlockSpec((tm, LANE), lambda i: (i, 0)),
            pl.BlockSpec((8, LANE), lambda i: (0, 0)),
        ],
        out_shape=[jax.ShapeDtypeStruct((T, LANE), F32), jax.ShapeDtypeStruct((8, LANE), F32)],
        scratch_shapes=[pltpu.VMEM((1, LANE), F32)],
        compiler_params=_cparams(("arbitrary",)),
        name="moe_router",
    )(h2, w_router_pad)


def _row_copy(src_hbm, src_row, dst_vmem, dst_row, sem):
    return pltpu.make_async_copy(src_hbm.at[pl.ds(src_row, 1), :], dst_vmem.at[pl.ds(dst_row, 1), :], sem)


def _expert_kernel(tile_expert_ref, n_tiles_ref, slot_token_ref, dest_ref, h_hbm, w1_ref, w3_ref, w2_ref,
                   y_hbm, rows_ref, hb_ref, stage_ref, gsems, ssems, *, tm, chunk):
    i = pl.program_id(0)
    f = pl.program_id(1)
    nf = pl.num_programs(1)
    n_live = n_tiles_ref[0]
    live = i < n_live
    cur = i % 2
    prev = 1 - cur

    def gather_chunk(tile, part, dst):
        for u in range(chunk):
            row = part * chunk + u
            _row_copy(h_hbm, slot_token_ref[tile * tm + row], rows_ref.at[dst], row, gsems.at[dst]).start()

    def scatter_copy(src, row, dest_row):
        return pltpu.make_async_copy(stage_ref.at[src, pl.ds(row, 1), :], y_hbm.at[pl.ds(dest_row, 1), :],
                                     ssems.at[src])

    def scatter_chunk(part):
        for u in range(chunk):
            row = part * chunk + u
            scatter_copy(prev, row, dest_ref[i * tm + row]).start()

    @pl.when((i == 0) & (f == 0))
    def _():
        stage_ref[1] = jnp.zeros(stage_ref.shape[1:], stage_ref.dtype)

        def issue(part, carry):
            gather_chunk(0, part, 0)
            return carry

        lax.fori_loop(0, nf, issue, 0)

    @pl.when((f == 0) & (i <= n_live))
    def _():
        def wait(r, carry):
            _row_copy(h_hbm, 0, rows_ref.at[cur], r, gsems.at[cur]).wait()
            return carry

        lax.fori_loop(0, tm, wait, 0, unroll=chunk)

    @pl.when((f == 0) & (i >= 1) & (i <= n_live + 1))
    def _():
        def wait(r, carry):
            scatter_copy(cur, r, 0).wait()
            return carry

        lax.fori_loop(0, tm, wait, 0, unroll=chunk)

    @pl.when(live & (f == 0))
    def _():
        hb_ref[...] = rows_ref[cur].astype(BF16)
        stage_ref[cur] = jnp.zeros(stage_ref.shape[1:], stage_ref.dtype)

    @pl.when(live)
    def _():
        gather_chunk(i + 1, f, prev)
        scatter_chunk(f)
        h = hb_ref[...]
        act = (_silu(_dot(h, w1_ref[0])) * _dot(h, w3_ref[0])).astype(BF16)
        stage_ref[cur] += _dot(act, w2_ref[0])

    @pl.when(i == n_live)
    def _():
        scatter_chunk(f)


def _expert_ffn(h2, tile_expert, n_tiles, slot_token, dest, w1, w3, w2, n_out_rows):
    tm, tf, nf = MOE_TM, MOE_TF, MOE_STEPS
    max_tiles = tile_expert.shape[0]

    def live_tile(i, nt):
        return jnp.minimum(i, nt[0] - 1)

    def f_idx(i, f, nt):
        return jnp.where(i < nt[0], f, nf - 1)

    grid_spec = pltpu.PrefetchScalarGridSpec(
        num_scalar_prefetch=4,
        grid=(max_tiles, nf),
        in_specs=[
            pl.BlockSpec(memory_space=pl.ANY),
            pl.BlockSpec((1, D_MODEL, tf), lambda i, f, te, nt, st, ds: (te[live_tile(i, nt)], 0, f_idx(i, f, nt))),
            pl.BlockSpec((1, D_MODEL, tf), lambda i, f, te, nt, st, ds: (te[live_tile(i, nt)], 0, f_idx(i, f, nt))),
            pl.BlockSpec((1, tf, D_MODEL), lambda i, f, te, nt, st, ds: (te[live_tile(i, nt)], f_idx(i, f, nt), 0)),
        ],
        out_specs=pl.BlockSpec(memory_space=pl.ANY),
        scratch_shapes=[
            pltpu.VMEM((2, tm, D_MODEL), F32),
            pltpu.VMEM((tm, D_MODEL), BF16),
            pltpu.VMEM((2, tm, D_MODEL), F32),
            pltpu.SemaphoreType.DMA((2,)),
            pltpu.SemaphoreType.DMA((2,)),
        ],
    )
    return pl.pallas_call(
        functools.partial(_expert_kernel, tm=tm, chunk=MOE_CHUNK),
        grid_spec=grid_spec,
        out_shape=jax.ShapeDtypeStruct((n_out_rows, D_MODEL), F32),
        compiler_params=_cparams(("arbitrary", "arbitrary")),
        name="moe_experts",
    )(tile_expert, n_tiles, slot_token, dest, h2, w1, w3, w2)


def _combine_kernel(y1_ref, y2_ref, x_ref, info_ref, mod_ref, g_ref, b_ref, o_ref):
    info = info_ref[...]
    y = info[:, INFO_W:INFO_W + 1] * y1_ref[...] + info[:, INFO_W + 1:INFO_W + 2] * y2_ref[...]
    z = DN_ALPHA * x_ref[...] + mod_ref[0, 5:6, :] * y
    o_ref[...] = _layer_norm(z, g_ref[...], b_ref[...])


def _moe_combine(y_tok, x2, info, mod_l, ln_g, ln_b, seq):
    T = x2.shape[0]
    tm = 512
    per_b = seq // tm
    row = pl.BlockSpec((tm, D_MODEL), lambda i: (i, 0))
    vec = pl.BlockSpec((1, D_MODEL), lambda i: (0, 0))
    return pl.pallas_call(
        _combine_kernel,
        grid=(T // tm,),
        in_specs=[
            row,
            pl.BlockSpec((tm, D_MODEL), lambda i: (T // tm + i, 0)),
            row,
            pl.BlockSpec((tm, LANE), lambda i: (i, 0)),
            pl.BlockSpec((1, N_MOD, D_MODEL), lambda i: (i // per_b, 0, 0)),
            vec, vec,
        ],
        out_specs=row,
        out_shape=jax.ShapeDtypeStruct((T, D_MODEL), F32),
        compiler_params=_cparams(("parallel",)),
        name="moe_combine_ln",
    )(y_tok, y_tok, x2, info, mod_l, ln_g.reshape(1, D_MODEL), ln_b.reshape(1, D_MODEL))


def _moe_sublayer(x2, h2, mod_l, w_router, w1, w3, w2, ln_g, ln_b, seq):
    T = x2.shape[0]
    tm = MOE_TM
    wr_pad = jnp.zeros((D_MODEL, LANE), BF16).at[:, :N_EXPERTS].set(w_router.astype(BF16))
    info, cnt = _router(h2, wr_pad)
    experts = info[:, INFO_I:INFO_I + TOP_K].astype(jnp.int32)
    ranks = info[:, INFO_R:INFO_R + TOP_K].astype(jnp.int32)
    counts = cnt[0, :N_EXPERTS].astype(jnp.int32)
    tiles_e = (counts + tm - 1) // tm
    tile_end = jnp.cumsum(tiles_e)
    tile_start = tile_end - tiles_e
    max_tiles = (T * TOP_K + N_EXPERTS * (tm - 1)) // tm + 2
    slots = (tile_start * tm)[experts] + ranks
    tile_ids = jnp.arange(max_tiles, dtype=jnp.int32)
    tile_expert = jnp.minimum(jnp.sum(tile_ids[:, None] >= tile_end[None, :], axis=1), N_EXPERTS - 1).astype(jnp.int32)
    n_tiles = tile_end[-1:].astype(jnp.int32)
    token_ids = jnp.broadcast_to(jnp.arange(T, dtype=jnp.int32)[:, None], (T, TOP_K))
    n_slots = max_tiles * tm
    spare = TOP_K * T + jnp.arange(n_slots, dtype=jnp.int32) % tm
    out_rows = jnp.arange(TOP_K, dtype=jnp.int32)[None, :] * T + token_ids
    slot_dest = spare.at[slots.reshape(-1)].set(out_rows.reshape(-1))
    slot_token = slot_dest % T
    dest = jnp.concatenate([spare[:tm], slot_dest[:n_slots - tm]])
    y_tok = _expert_ffn(h2, tile_expert, n_tiles, slot_token, dest, w1, w3, w2, TOP_K * T + tm)
    return _moe_combine(y_tok, x2, info, mod_l, ln_g, ln_b, seq)


def _pack_w_in(w_in_l):
    low = jnp.zeros((D_MODEL, N_PROJ - U_LOW * LANE), w_in_l.dtype).at[:, :GLA_RANK].set(w_in_l[:, _REF_LOW0:_REF_G0])
    a_cols = [w_in_l[:, part * A_WIDTH + g * W4:part * A_WIDTH + (g + 1) * W4]
              for g in range(len(DIL_GROUPS)) for part in range(3)]
    return jnp.concatenate(a_cols + [w_in_l[:, 3 * A_WIDTH:_REF_LOW0], w_in_l[:, _REF_G0:], low], axis=1).astype(BF16)


def _pack_w_up(w_up_l):
    pad = jnp.zeros((LANE, GLA_DK), F32).at[:GLA_RANK].set(w_up_l)
    hi = pad.astype(BF16)
    lo = (pad - hi.astype(F32)).astype(BF16)
    return hi, lo


def kernel(x, c, w_ada, b_ada, ln_g, ln_b, w_in, w_alpha_up, b_alpha, gla_norm_g, b_merge, w_branch_a,
           w_branch_b, w_out, ffn_w1, ffn_w3, ffn_w2, w_router, moe_w1, moe_w3, moe_w2):
    batch, seq, _ = x.shape
    T = batch * seq
    x2 = x.reshape(T, D_MODEL)
    mod = _adaln_mod(c, w_ada, b_ada)
    cast_block = {"w1": (1, D_MODEL, MXU_DIM), "w3": (1, D_MODEL, MXU_DIM), "w2": (1, MXU_DIM, D_MODEL)}
    to_cast = [(l, name, w[l // 2]) for l in range(DEPTH) if l % 2 == 1
               for name, w in (("w1", moe_w1), ("w2", moe_w2), ("w3", moe_w3))]
    moe_bf16 = {}

    def next_side(before_layer):
        if to_cast and to_cast[0][0] >= before_layer:
            layer, name, w = to_cast.pop(0)
            return (layer, name), (w, cast_block[name])
        return None, None

    for l in range(DEPTH):
        mod_l = mod[l]
        key, side = next_side(l)
        proj, cast = _in_projection(x2, mod_l, _pack_w_in(w_in[l]), seq, side)
        if key is not None:
            moe_bf16[key] = cast
        groups = [_dilated_group(proj, batch, seq, g) for g in range(len(DIL_GROUPS))]
        w_up_hi, w_up_lo = _pack_w_up(w_alpha_up[l])
        y_b = _gla_branch(proj, batch, seq, w_up_hi, w_up_lo, b_alpha[l], gla_norm_g[l])
        mixed = _branch_mix([g[0] for g in groups], [g[1] for g in groups], y_b, proj, b_merge[l],
                            w_branch_a[l].astype(BF16), w_branch_b[l].astype(BF16))
        moe_layer = l % 2 == 1
        x2, h2 = _out_projection(mixed, x2, mod_l, w_out[l].astype(BF16), ln_g[l, 0], ln_b[l, 0], seq, moe_layer)
        e = l // 2
        if moe_layer:
            to_cast = [item for item in to_cast if item[0] != l]
            w1, w3, w2 = (moe_bf16[(l, name)] if (l, name) in moe_bf16 else w[e].astype(BF16)
                          for name, w in (("w1", moe_w1), ("w3", moe_w3), ("w2", moe_w2)))
            x2 = _moe_sublayer(x2, h2, mod_l, w_router[e], w1, w3, w2, ln_g[l, 1], ln_b[l, 1], seq)
        else:
            key, side = next_side(l + 1)
            x2, cast = _dense_ffn(x2, mod_l, ffn_w1[e].astype(BF16), ffn_w3[e].astype(BF16), ffn_w2[e].astype(BF16),
                                  ln_g[l, 1], ln_b[l, 1], seq, side)
            if key is not None:
                moe_bf16[key] = cast
    return x2.reshape(batch, seq, D_MODEL)
```

```python
import functools

import jax
import jax.numpy as jnp
from jax import lax
from jax.experimental import pallas as pl
from jax.experimental.pallas import tpu as pltpu

F32 = jnp.float32
BF16 = jnp.bfloat16

D_MODEL = 2048
DEPTH = 2
HEAD_DIM = 128
DIL_GROUPS = ((128, 1), (512, 4), (2048, 16))
HEADS_PER_GROUP = 4
A_HEADS = HEADS_PER_GROUP * len(DIL_GROUPS)
A_WIDTH = A_HEADS * HEAD_DIM
A_OUT = HEADS_PER_GROUP * HEAD_DIM
ALIBI_MAX_EXP = 8.0
GLA_HEADS = 4
GLA_DK = D_MODEL // 2
GLA_DV = D_MODEL
GLA_HK = GLA_DK // GLA_HEADS
GLA_HV = GLA_DV // GLA_HEADS
GLA_RANK = 16
GLA_TAU = 16.0
GLA_CHUNK = 64
N_BRANCH = 2
FFN_DENSE = 5632
N_EXPERTS = 8
TOP_K = 2
FFN_EXPERT = 7168
N_MOD = 6
LN_EPS = 1e-5
NORM_EPS = 1e-6
DN_ALPHA = (2 * DEPTH) ** 0.25

LANE = 128
MXU_DIM = 256
VMEM_LIMIT_MB = 56

U_GROUP = 12
U_A_END = 36
U_BQ, U_BK, U_BV, U_BR = 36, 44, 52, 68
U_GA, U_GB = 84, 100
U_LOW = 116
N_UNITS = 120
N_PROJ = N_UNITS * LANE
BAND = 128
W4 = HEADS_PER_GROUP * HEAD_DIM
PERM_TILE = 1024

_REF_LOW0 = 3 * A_WIDTH + 2 * GLA_DK + 2 * GLA_DV
_REF_G0 = _REF_LOW0 + GLA_RANK


def _cparams(semantics, vmem_mb=VMEM_LIMIT_MB):
    return pltpu.CompilerParams(dimension_semantics=semantics, vmem_limit_bytes=vmem_mb << 20)


def _dot(a, b):
    return jnp.dot(a, b, preferred_element_type=F32)


def _dot_nt(a, b):
    return lax.dot_general(a, b, (((1,), (1,)), ((), ())), preferred_element_type=F32)


def _dot_tn(a, b):
    return lax.dot_general(a, b, (((0,), (0,)), ((), ())), preferred_element_type=F32)


def _sigmoid(x):
    return 1.0 / (1.0 + jnp.exp(-x))


def _silu(x):
    return x * _sigmoid(x)


def _layer_norm(z, g, b):
    mu = jnp.mean(z, axis=-1, keepdims=True)
    zc = z - mu
    var = jnp.mean(zc * zc, axis=-1, keepdims=True)
    return zc * lax.rsqrt(var + LN_EPS) * g + b


def _mod_kernel(c_ref, w_ref, b_ref, o_ref):
    s = _silu(c_ref[...]).astype(BF16)
    o_ref[0] = _dot(s, w_ref[0].astype(BF16)) + b_ref[0]


def _adaln_mod(c, w_ada, b_ada):
    B = c.shape[0]
    rows = 8
    tn = 1024
    c_pad = jnp.zeros((rows, D_MODEL), F32).at[:B].set(c)
    n_out = N_MOD * D_MODEL
    out = pl.pallas_call(
        _mod_kernel,
        grid=(DEPTH, n_out // tn),
        in_specs=[
            pl.BlockSpec((rows, D_MODEL), lambda l, j: (0, 0)),
            pl.BlockSpec((1, D_MODEL, tn), lambda l, j: (l, 0, j)),
            pl.BlockSpec((1, 1, tn), lambda l, j: (l, 0, j)),
        ],
        out_specs=pl.BlockSpec((1, rows, tn), lambda l, j: (l, 0, j)),
        out_shape=jax.ShapeDtypeStruct((DEPTH, rows, n_out), F32),
        compiler_params=_cparams(("arbitrary", "arbitrary")),
        name="adaln_mod",
    )(c_pad, w_ada, b_ada.reshape(DEPTH, 1, n_out))
    return out[:, :B].reshape(DEPTH, B, N_MOD, D_MODEL)


def _unit_dilation(unit):
    return DIL_GROUPS[unit // U_GROUP][1] if unit < U_A_END else 1


def _side_cast_specs(src, block, steps_inner):
    n_e, n_r, n_c = src.shape
    _, br, bc = block
    per_e = (n_r // br) * (n_c // bc)
    last = n_e * per_e - 1

    def index(i, j):
        s = jnp.minimum(i * steps_inner + j, last)
        rem = s % per_e
        return s // per_e, rem // (n_c // bc), rem % (n_c // bc)

    return pl.BlockSpec(block, index), last + 1


def _inproj_kernel(*refs, tm, tn, side):
    if side:
        x_ref, mod_ref, w_ref, side_ref, o_ref, side_out_ref, h_ref, slab_ref = refs
        side_out_ref[...] = side_ref[...].astype(side_out_ref.dtype)
    else:
        x_ref, mod_ref, w_ref, o_ref, h_ref, slab_ref = refs
    j = pl.program_id(1)

    @pl.when(j == 0)
    def _():
        shift = mod_ref[0, 0:1, :]
        scale = mod_ref[0, 1:2, :]
        h_ref[...] = (x_ref[...] * (1.0 + scale) + shift).astype(BF16)

    acc = _dot(h_ref[...], w_ref[...])
    parts = tn // W4
    tile_dils = [[_unit_dilation((jj * tn + p * W4) // LANE) for p in range(parts)] for jj in range(N_PROJ // tn)]
    perm_tiles = [jj for jj, ds in enumerate(tile_dils) if any(d > 1 for d in ds)]
    is_perm = functools.reduce(jnp.logical_or, [j == jj for jj in perm_tiles])

    @pl.when(jnp.logical_not(is_perm))
    def _():
        o_ref[...] = acc.astype(o_ref.dtype)

    for jj in perm_tiles:
        @pl.when(j == jj)
        def _(jj=jj):
            for p, dil in enumerate(tile_dils[jj]):
                c0 = p * W4
                if dil == 1:
                    o_ref[:, c0:c0 + W4] = acc[:, c0:c0 + W4].astype(o_ref.dtype)
                    continue
                n = tm // dil
                for s in range(W4 // LANE):
                    slab_ref[s] = acc[:, c0 + s * LANE:c0 + (s + 1) * LANE]
                for s in range(W4 // LANE):
                    for r in range(dil):
                        o_ref[r * n:(r + 1) * n, c0 + s * LANE:c0 + (s + 1) * LANE] = (
                            slab_ref[s, pl.ds(r, n, stride=dil), :].astype(o_ref.dtype))


def _in_projection(x2, mod_l, w_cat, seq, side=None):
    T = x2.shape[0]
    tm = PERM_TILE
    tn = 1024
    per_b = seq // tm
    grid = (T // tm, N_PROJ // tn)
    in_specs = [
        pl.BlockSpec((tm, D_MODEL), lambda i, j: (i, 0)),
        pl.BlockSpec((1, N_MOD, D_MODEL), lambda i, j: (i // per_b, 0, 0)),
        pl.BlockSpec((D_MODEL, tn), lambda i, j: (0, j)),
    ]
    out_specs = [pl.BlockSpec((tm, tn), lambda i, j: (i, j))]
    out_shape = [jax.ShapeDtypeStruct((T, N_PROJ), BF16)]
    args = [x2, mod_l, w_cat]
    if side is not None:
        src, block = side
        spec, n_blocks = _side_cast_specs(src, block, grid[1])
        assert n_blocks <= grid[0] * grid[1]
        in_specs.append(spec)
        out_specs.append(spec)
        out_shape.append(jax.ShapeDtypeStruct(src.shape, BF16))
        args.append(src)
    outs = pl.pallas_call(
        functools.partial(_inproj_kernel, tm=tm, tn=tn, side=side is not None),
        grid=grid,
        in_specs=in_specs,
        out_specs=out_specs,
        out_shape=out_shape,
        scratch_shapes=[pltpu.VMEM((tm, D_MODEL), BF16), pltpu.VMEM((W4 // LANE, tm, LANE), F32)],
        compiler_params=_cparams(("arbitrary", "arbitrary")),
        name="in_projection",
    )(*args)
    return (outs[0], outs[1]) if side is not None else (outs[0], None)


LSE_LANES = LANE // HEADS_PER_GROUP


def _attn_kernel(*refs, c, nt, bias_scale, has_prev):
    if has_prev:
        q_ref, kp_ref, k_ref, vp_ref, v_ref, o_ref, lse_ref = refs
        first_run = pl.program_id(1) == 0
    else:
        q_ref, k_ref, v_ref, o_ref, lse_ref = refs
    qi = lax.broadcasted_iota(jnp.int32, (BAND, BAND), 0)
    ki = lax.broadcasted_iota(jnp.int32, (BAND, BAND), 1)
    valid_prev = ki >= qi
    valid_cur = ki <= qi
    delta_prev = (qi + BAND - ki).astype(F32)
    delta_cur = (qi - ki).astype(F32)
    neg_inf = jnp.float32(-jnp.inf)
    sm_scale = HEAD_DIM ** -0.5

    def rows(sb):
        if c >= BAND:
            per = c // BAND
            return sb // per, slice((sb % per) * BAND, (sb % per + 1) * BAND)
        per = BAND // c
        return slice(sb * per, (sb + 1) * per), slice(None)

    def load(ref, sb, cs):
        t, r = rows(sb)
        return ref[t, r, cs].reshape(BAND, cs.stop - cs.start)

    def store(ref, sb, cs, val):
        t, r = rows(sb)
        if c < BAND:
            val = val.reshape(BAND // c, c, cs.stop - cs.start)
        ref[t, r, cs] = val

    heads = range(HEADS_PER_GROUP)
    cols = [slice(hh * HEAD_DIM, (hh + 1) * HEAD_DIM) for hh in heads]
    bias_cur = jnp.concatenate([jnp.where(valid_cur, -bias_scale[hh] * delta_cur, neg_inf) for hh in heads], axis=0)
    bias_prev = jnp.concatenate([jnp.where(valid_prev, -bias_scale[hh] * delta_prev, neg_inf) for hh in heads], axis=0)
    head_rows = [slice(hh * BAND, (hh + 1) * BAND) for hh in heads]
    for sb in range(nt * c // BAND):
        with_prev = sb > 0 or has_prev
        qs = [load(q_ref, sb, cs) for cs in cols]
        v_cur = [load(v_ref, sb, cs) for cs in cols]
        s_cur = jnp.concatenate([_dot_nt(qs[hh], load(k_ref, sb, cols[hh])) for hh in heads], axis=0)
        s_cur = s_cur * sm_scale + bias_cur
        m = jnp.max(s_cur, axis=1, keepdims=True)
        if with_prev:
            if sb > 0:
                k_prev = [load(k_ref, sb - 1, cs) for cs in cols]
                v_prev = [load(v_ref, sb - 1, cs) for cs in cols]
            else:
                k_prev = [kp_ref[0, :, cs] for cs in cols]
                v_prev = [vp_ref[0, :, cs] for cs in cols]
            s_prev = jnp.concatenate([_dot_nt(qs[hh], k_prev[hh]) for hh in heads], axis=0)
            s_prev = s_prev * sm_scale + bias_prev
            if sb == 0:
                s_prev = jnp.where(first_run, neg_inf, s_prev)
            m = jnp.maximum(m, jnp.max(s_prev, axis=1, keepdims=True))
            p_prev = jnp.exp(s_prev - m)
        p_cur = jnp.exp(s_cur - m)
        den = jnp.sum(p_cur, axis=1, keepdims=True)
        if with_prev:
            den = den + jnp.sum(p_prev, axis=1, keepdims=True)
        inv_den = 1.0 / den
        lse = m + jnp.log(den)
        p_cur = p_cur.astype(BF16)
        if with_prev:
            p_prev = p_prev.astype(BF16)
        for hh in heads:
            o = _dot(p_cur[head_rows[hh]], v_cur[hh])
            if with_prev:
                o = o + _dot(p_prev[head_rows[hh]], v_prev[hh])
            store(o_ref, sb, cols[hh], o * inv_den[head_rows[hh]])
            store(lse_ref, sb, slice(hh * LSE_LANES, (hh + 1) * LSE_LANES),
                  jnp.broadcast_to(lse[head_rows[hh]], (BAND, LSE_LANES)))


def _dilated_group(proj, batch, seq, g):
    _, dil = DIL_GROUPS[g]
    T = batch * seq
    tiles = T // PERM_TILE
    per_b = seq // PERM_TILE
    qcol, kcol, vcol = (g * U_GROUP * LANE // W4 + i for i in range(3))
    slopes = [2.0 ** (-ALIBI_MAX_EXP * (g * HEADS_PER_GROUP + h + 1) / A_HEADS) * dil
              for h in range(HEADS_PER_GROUP)]
    if dil == 1:
        c, nt = PERM_TILE, 1
        runs = proj.reshape(tiles, PERM_TILE, N_PROJ)
        bands = proj.reshape(T // BAND, BAND, N_PROJ)
        per_run = PERM_TILE // BAND
        cur = lambda col: pl.BlockSpec((1, c, W4), lambda b, n: (b * per_b + n, 0, col))
        prev = lambda col: pl.BlockSpec(
            (1, BAND, W4), lambda b, n: (jnp.maximum((b * per_b + n) * per_run - 1, 0), 0, col))
        grid = (batch, per_b)
        in_specs = [cur(qcol), prev(kcol), cur(kcol), prev(vcol), cur(vcol)]
        args = (runs, bands, runs, bands, runs)
        out_specs = [pl.BlockSpec((1, c, W4), lambda b, n: (b * per_b + n, 0, 0)),
                     pl.BlockSpec((1, c, LANE), lambda b, n: (b * per_b + n, 0, 0))]
        out_shape = [jax.ShapeDtypeStruct((tiles, c, W4), F32), jax.ShapeDtypeStruct((tiles, c, LANE), F32)]
        semantics = ("parallel", "arbitrary")
    else:
        c, nt = PERM_TILE // dil, per_b
        runs = proj.reshape(tiles, dil, c, N_PROJ)
        cur = lambda col: pl.BlockSpec((nt, None, c, W4), lambda b, r: (b, r, 0, col))
        grid = (batch, dil)
        in_specs = [cur(qcol), cur(kcol), cur(vcol)]
        args = (runs, runs, runs)
        out_specs = [pl.BlockSpec((nt, None, c, W4), lambda b, r: (b, r, 0, 0)),
                     pl.BlockSpec((nt, None, c, LANE), lambda b, r: (b, r, 0, 0))]
        out_shape = [jax.ShapeDtypeStruct((tiles, dil, c, W4), F32),
                     jax.ShapeDtypeStruct((tiles, dil, c, LANE), F32)]
        semantics = ("parallel", "parallel")
    o, lse = pl.pallas_call(
        functools.partial(_attn_kernel, c=c, nt=nt, bias_scale=slopes, has_prev=dil == 1),
        grid=grid,
        in_specs=in_specs,
        out_specs=out_specs,
        out_shape=out_shape,
        compiler_params=_cparams(semantics),
        name=f"dilated_attn_g{g}",
    )(*args)
    return o.reshape(T, W4), lse.reshape(T, LANE)


GLA_GROUP = 256


GLA_HEADS_PER_STEP = 2


def _gla_kernel(*refs, tb, hp):
    q_ref, k_ref = refs[:2]
    v_refs, r_refs = refs[2:2 + hp], refs[2 + hp:2 + 2 * hp]
    low_ref, whi_ref, wlo_ref, balpha_ref, gnorm_ref, y_ref, state_ref = refs[2 + 2 * hp:]

    @pl.when(pl.program_id(2) == 0)
    def _():
        state_ref[...] = jnp.zeros_like(state_ref)

    C = GLA_CHUNK
    G = GLA_GROUP
    kw = hp * GLA_HK
    low = low_ref[...]
    z = _dot(low, whi_ref[...]) + _dot(low, wlo_ref[...]) + balpha_ref[...]
    log_a = (jnp.minimum(z, 0.0) - jnp.log(1.0 + jnp.exp(-jnp.abs(z)))) * (1.0 / GLA_TAU)

    ri = lax.broadcasted_iota(jnp.int32, (G, G), 0)
    ci = lax.broadcasted_iota(jnp.int32, (G, G), 1)
    same_chunk = (ri // C) == (ci // C)
    causal = same_chunk & (ci <= ri)
    tril = causal.astype(BF16)
    sel_r = lax.broadcasted_iota(jnp.int32, (G, (G // C) * LANE), 0)
    sel_c = lax.broadcasted_iota(jnp.int32, (G, (G // C) * LANE), 1)
    chunk_sel = ((sel_r // C) == (sel_c // LANE)).astype(BF16)

    groups = [slice(g * G, (g + 1) * G) for g in range(tb // G)]
    head_cols = [slice(h * GLA_HK, (h + 1) * GLA_HK) for h in range(hp)]
    gcs, decay_cols = [], {}
    for gi, rows in enumerate(groups):
        la = log_a[rows]
        la_hi = la.astype(BF16)
        la_lo = (la - la_hi.astype(F32)).astype(BF16)
        gcs.append(_dot(tril, la_hi) + _dot(tril, la_lo))
        for h, hc in enumerate(head_cols):
            decay_cols[gi, h] = jnp.exp(_dot_tn(la_hi[:, hc], chunk_sel) + _dot_tn(la_lo[:, hc], chunk_sel))
    gc = jnp.concatenate(gcs, axis=0)
    chunks = [slice(c * C, (c + 1) * C) for c in range(tb // C)]
    g_mid = jnp.concatenate([jnp.broadcast_to(gc[cr][C // 2 - 1:C // 2], (C, kw)) for cr in chunks], axis=0)
    g_end = jnp.concatenate([jnp.broadcast_to(gc[cr][C - 1:C], (C, kw)) for cr in chunks], axis=0)
    q_all = q_ref[...].astype(F32) * (GLA_HK ** -0.5)
    k_all = k_ref[...].astype(F32)
    qe = (q_all * jnp.exp(gc - g_mid)).astype(BF16)
    ke = (k_all * jnp.exp(g_mid - gc)).astype(BF16)
    q_in = (q_all * jnp.exp(gc)).astype(BF16)
    k_out = (k_all * jnp.exp(g_end - gc)).astype(BF16)

    for h, hc in enumerate(head_cols):
        v_ref = v_refs[h]
        state = state_ref[h]
        outs = []
        for gi, rows in enumerate(groups):
            o_inter = []
            for c in range(G // C):
                cr = slice(gi * G + c * C, gi * G + (c + 1) * C)
                update = _dot_tn(k_out[cr, hc], v_ref[cr, :])
                o_inter.append(_dot(q_in[cr, hc], state.astype(BF16)))
                decay = decay_cols[gi, h][:, c * LANE:(c + 1) * LANE]
                state = jnp.concatenate(
                    [state[:, j * LANE:(j + 1) * LANE] * decay for j in range(GLA_HV // LANE)], axis=1) + update
            att = jnp.where(causal, _dot_nt(qe[rows, hc], ke[rows, hc]), 0.0).astype(BF16)
            outs.append(_dot(att, v_ref[rows, :]) + jnp.concatenate(o_inter, axis=0))
        state_ref[h] = state
        o = jnp.concatenate(outs, axis=0)
        ms = jnp.mean(o * o, axis=-1, keepdims=True)
        o = o * lax.rsqrt(ms + NORM_EPS) * gnorm_ref[...]
        y_ref[:, h * GLA_HV:(h + 1) * GLA_HV] = (_silu(r_refs[h][...].astype(F32)) * o).astype(y_ref.dtype)


def _gla_branch(proj, batch, seq, w_up_hi, w_up_lo, b_alpha, gnorm):
    T = batch * seq
    tb = 512
    hp = GLA_HEADS_PER_STEP
    kw = hp * GLA_HK
    per_b = seq // tb

    def rows(b, n):
        return b * per_b + n

    def per_head(unit, h):
        return pl.BlockSpec((tb, GLA_HV), lambda b, hb, n: (rows(b, n), unit * LANE // GLA_HV + hb * hp + h))

    in_specs = [
        pl.BlockSpec((tb, kw), lambda b, hb, n: (rows(b, n), U_BQ * LANE // kw + hb)),
        pl.BlockSpec((tb, kw), lambda b, hb, n: (rows(b, n), U_BK * LANE // kw + hb)),
        *[per_head(U_BV, h) for h in range(hp)],
        *[per_head(U_BR, h) for h in range(hp)],
        pl.BlockSpec((tb, LANE), lambda b, hb, n: (rows(b, n), U_LOW)),
        pl.BlockSpec((LANE, kw), lambda b, hb, n: (0, hb)),
        pl.BlockSpec((LANE, kw), lambda b, hb, n: (0, hb)),
        pl.BlockSpec((1, kw), lambda b, hb, n: (0, hb)),
        pl.BlockSpec((1, GLA_HV), lambda b, hb, n: (0, 0)),
    ]
    return pl.pallas_call(
        functools.partial(_gla_kernel, tb=tb, hp=hp),
        grid=(batch, GLA_HEADS // hp, per_b),
        in_specs=in_specs,
        out_specs=pl.BlockSpec((tb, hp * GLA_HV), lambda b, hb, n: (rows(b, n), hb)),
        out_shape=jax.ShapeDtypeStruct((T, GLA_DV), BF16),
        scratch_shapes=[pltpu.VMEM((hp, GLA_HK, GLA_HV), F32)],
        compiler_params=_cparams(("parallel", "parallel", "arbitrary")),
        name="gla_branch",
    )(*([proj] * (3 + 2 * hp)), w_up_hi, w_up_lo, b_alpha.reshape(1, GLA_DK), gnorm.reshape(1, GLA_HV))


def _branch_kernel(o0_ref, o1_ref, o2_ref, l0_ref, l1_ref, l2_ref, yb_ref, ga_ref, gb_ref, bm_ref,
                   wa_ref, wb_ref, out_ref, ya_ref, oslab_ref, lslab_ref, *, tm):
    @pl.when(pl.program_id(1) == 0)
    def _():
        for gi, (o_ref, l_ref) in enumerate(((o1_ref, l1_ref), (o2_ref, l2_ref))):
            dil = DIL_GROUPS[gi + 1][1]
            n = tm // dil
            for r in range(dil):
                src = slice(r * n, (r + 1) * n)
                dst = pl.ds(r, n, stride=dil)
                for h in range(HEADS_PER_GROUP):
                    oslab_ref[gi, h, dst, :] = o_ref[src, h * HEAD_DIM:(h + 1) * HEAD_DIM]
                lslab_ref[gi, dst, :] = l_ref[src, :]
        l0, l1, l2 = l0_ref[...], lslab_ref[0], lslab_ref[1]
        m = jnp.maximum(jnp.maximum(l0, l1), l2)
        e0, e1, e2 = jnp.exp(l0 - m), jnp.exp(l1 - m), jnp.exp(l2 - m)
        inv = 1.0 / (e0 + e1 + e2)
        w0, w1, w2 = e0 * inv, e1 * inv, e2 * inv
        for h in range(HEADS_PER_GROUP):
            cs = slice(h * HEAD_DIM, (h + 1) * HEAD_DIM)
            lane = slice(h * LSE_LANES, h * LSE_LANES + 1)
            ya = o0_ref[:, cs] * w0[:, lane] + oslab_ref[0, h] * w1[:, lane] + oslab_ref[1, h] * w2[:, lane]
            ya_ref[:, cs] = ya.astype(BF16)

    pa = _dot(ya_ref[...], wa_ref[...])
    pb = _dot(yb_ref[...], wb_ref[...])
    gate_a = _sigmoid(ga_ref[...].astype(F32) + bm_ref[0:1, :])
    gate_b = _sigmoid(gb_ref[...].astype(F32) + bm_ref[1:2, :])
    out_ref[...] = (gate_a * pa + gate_b * pb).astype(out_ref.dtype)


def _branch_mix(outs, lses, y_b, proj, b_merge, wa, wb):
    T = y_b.shape[0]
    tm, tn = PERM_TILE, 512
    row = lambda i, j: (i, 0)
    n_dilated = len(DIL_GROUPS) - 1
    return pl.pallas_call(
        functools.partial(_branch_kernel, tm=tm),
        grid=(T // tm, D_MODEL // tn),
        in_specs=[pl.BlockSpec((tm, A_OUT), row)] * 3 + [pl.BlockSpec((tm, LANE), row)] * 3 + [
            pl.BlockSpec((tm, GLA_DV), row),
            pl.BlockSpec((tm, tn), lambda i, j: (i, U_GA * LANE // tn + j)),
            pl.BlockSpec((tm, tn), lambda i, j: (i, U_GB * LANE // tn + j)),
            pl.BlockSpec((N_BRANCH, tn), lambda i, j: (0, j)),
            pl.BlockSpec((A_OUT, tn), lambda i, j: (0, j)),
            pl.BlockSpec((GLA_DV, tn), lambda i, j: (0, j)),
        ],
        out_specs=pl.BlockSpec((tm, tn), lambda i, j: (i, j)),
        out_shape=jax.ShapeDtypeStruct((T, D_MODEL), BF16),
        scratch_shapes=[
            pltpu.VMEM((tm, A_OUT), BF16),
            pltpu.VMEM((n_dilated, HEADS_PER_GROUP, tm, HEAD_DIM), F32),
            pltpu.VMEM((n_dilated, tm, LANE), F32),
        ],
        compiler_params=_cparams(("parallel", "arbitrary")),
        name="branch_mix",
    )(*outs, *lses, y_b, proj, proj, b_merge, wa, wb)


def _outproj_kernel(mixed_ref, x_ref, mod_ref, w_ref, g_ref, b_ref, *out_refs, emit_h, tm, parts):
    gate = mod_ref[0, 2:3, :]
    for p in range(parts):
        rs = slice(p * (tm // parts), (p + 1) * (tm // parts))
        y = _dot(mixed_ref[rs, :], w_ref[...])
        xn = _layer_norm(DN_ALPHA * x_ref[rs, :] + gate * y, g_ref[...], b_ref[...])
        out_refs[0][rs, :] = xn
        if emit_h:
            out_refs[1][rs, :] = xn * (1.0 + mod_ref[0, 4:5, :]) + mod_ref[0, 3:4, :]


def _out_projection(mixed, x2, mod_l, w_out, ln_g, ln_b, seq, emit_h):
    T = x2.shape[0]
    tm, parts = 512, 2
    per_b = seq // tm
    row = pl.BlockSpec((tm, D_MODEL), lambda i: (i, 0))
    vec = pl.BlockSpec((1, D_MODEL), lambda i: (0, 0))
    n_out = 2 if emit_h else 1
    outs = pl.pallas_call(
        functools.partial(_outproj_kernel, emit_h=emit_h, tm=tm, parts=parts),
        grid=(T // tm,),
        in_specs=[
            row, row,
            pl.BlockSpec((1, N_MOD, D_MODEL), lambda i: (i // per_b, 0, 0)),
            pl.BlockSpec((D_MODEL, D_MODEL), lambda i: (0, 0), pipeline_mode=pl.Buffered(1)),
            vec, vec,
        ],
        out_specs=[row] * n_out,
        out_shape=[jax.ShapeDtypeStruct((T, D_MODEL), F32)] * n_out,
        compiler_params=_cparams(("parallel",)),
        name="out_projection_ln",
    )(mixed, x2, mod_l, w_out, ln_g.reshape(1, D_MODEL), ln_b.reshape(1, D_MODEL))
    return outs if emit_h else (outs[0], None)


def _ffn_kernel(*refs, side):
    if side:
        x_ref, mod_ref, w1_ref, w3_ref, w2_ref, g_ref, b_ref, side_ref, o_ref, side_out_ref, h_ref, acc_ref = refs
        side_out_ref[...] = side_ref[...].astype(side_out_ref.dtype)
    else:
        x_ref, mod_ref, w1_ref, w3_ref, w2_ref, g_ref, b_ref, o_ref, h_ref, acc_ref = refs
    f = pl.program_id(1)

    @pl.when(f == 0)
    def _():
        h_ref[...] = (x_ref[...] * (1.0 + mod_ref[0, 4:5, :]) + mod_ref[0, 3:4, :]).astype(BF16)
        acc_ref[...] = jnp.zeros_like(acc_ref)

    h = h_ref[...]
    act = (_silu(_dot(h, w1_ref[...])) * _dot(h, w3_ref[...])).astype(BF16)
    acc_ref[...] += _dot(act, w2_ref[...])

    @pl.when(f == pl.num_programs(1) - 1)
    def _():
        z = DN_ALPHA * x_ref[...] + mod_ref[0, 5:6, :] * acc_ref[...]
        o_ref[...] = _layer_norm(z, g_ref[...], b_ref[...])


def _dense_ffn(x2, mod_l, w1, w3, w2, ln_g, ln_b, seq, side=None):
    T = x2.shape[0]
    tm, tf = 512, 512
    per_b = seq // tm
    grid = (T // tm, FFN_DENSE // tf)
    row = pl.BlockSpec((tm, D_MODEL), lambda i, f: (i, 0))
    vec = pl.BlockSpec((1, D_MODEL), lambda i, f: (0, 0))
    in_specs = [
        row,
        pl.BlockSpec((1, N_MOD, D_MODEL), lambda i, f: (i // per_b, 0, 0)),
        pl.BlockSpec((D_MODEL, tf), lambda i, f: (0, f)),
        pl.BlockSpec((D_MODEL, tf), lambda i, f: (0, f)),
        pl.BlockSpec((tf, D_MODEL), lambda i, f: (f, 0)),
        vec, vec,
    ]
    out_specs = [row]
    out_shape = [jax.ShapeDtypeStruct((T, D_MODEL), F32)]
    args = [x2, mod_l, w1, w3, w2, ln_g.reshape(1, D_MODEL), ln_b.reshape(1, D_MODEL)]
    if side is not None:
        src, block = side
        spec, n_blocks = _side_cast_specs(src, block, grid[1])
        assert n_blocks <= grid[0] * grid[1]
        in_specs.append(spec)
        out_specs.append(spec)
        out_shape.append(jax.ShapeDtypeStruct(src.shape, BF16))
        args.append(src)
    outs = pl.pallas_call(
        functools.partial(_ffn_kernel, side=side is not None),
        grid=grid,
        in_specs=in_specs,
        out_specs=out_specs,
        out_shape=out_shape,
        scratch_shapes=[pltpu.VMEM((tm, D_MODEL), BF16), pltpu.VMEM((tm, D_MODEL), F32)],
        compiler_params=_cparams(("arbitrary", "arbitrary")),
        name="dense_ffn_ln",
    )(*args)
    return (outs[0], outs[1]) if side is not None else (outs[0], None)


MOE_TF = 1024
MOE_STEPS = FFN_EXPERT // MOE_TF
MOE_CHUNK = 80
MOE_TM = MOE_STEPS * MOE_CHUNK
INFO_I, INFO_W, INFO_R = 0, 2, 4


def _router_kernel(h_ref, wr_ref, info_ref, cnt_ref, carry_ref, *, tm):
    @pl.when(pl.program_id(0) == 0)
    def _():
        carry_ref[...] = jnp.zeros_like(carry_ref)

    logits = _dot(h_ref[...].astype(BF16), wr_ref[...])
    lane = lax.broadcasted_iota(jnp.int32, (tm, LANE), 1).astype(F32)
    neg_inf = jnp.float32(-jnp.inf)
    lg = jnp.where(lane < N_EXPERTS, logits, neg_inf)
    m1 = jnp.max(lg, axis=1, keepdims=True)
    i1 = jnp.min(jnp.where(lg == m1, lane, float(LANE)), axis=1, keepdims=True)
    lg2 = jnp.where(lane == i1, neg_inf, lg)
    m2 = jnp.max(lg2, axis=1, keepdims=True)
    i2 = jnp.min(jnp.where(lg2 == m2, lane, float(LANE)), axis=1, keepdims=True)
    e = jnp.exp(m2 - m1)
    w1 = 1.0 / (1.0 + e)
    w2 = e / (1.0 + e)
    oh1 = lane == i1
    oh2 = lane == i2
    onehot = (oh1 | oh2).astype(BF16)
    ri = lax.broadcasted_iota(jnp.int32, (tm, tm), 0)
    ci = lax.broadcasted_iota(jnp.int32, (tm, tm), 1)
    strict_lower = (ci < ri).astype(BF16)
    rank = _dot(strict_lower, onehot) + carry_ref[...]
    r1 = jnp.sum(jnp.where(oh1, rank, 0.0), axis=1, keepdims=True)
    r2 = jnp.sum(jnp.where(oh2, rank, 0.0), axis=1, keepdims=True)
    carry = carry_ref[...] + jnp.sum(onehot.astype(F32), axis=0, keepdims=True)
    carry_ref[...] = carry
    cnt_ref[...] = jnp.broadcast_to(carry, cnt_ref.shape)
    info = jnp.where(lane == INFO_I, i1, 0.0)
    info = jnp.where(lane == INFO_I + 1, i2, info)
    info = jnp.where(lane == INFO_W, w1, info)
    info = jnp.where(lane == INFO_W + 1, w2, info)
    info = jnp.where(lane == INFO_R, r1, info)
    info = jnp.where(lane == INFO_R + 1, r2, info)
    info_ref[...] = info


def _router(h2, w_router_pad):
    T = h2.shape[0]
    tm = 512
    return pl.pallas_call(
        functools.partial(_router_kernel, tm=tm),
        grid=(T // tm,),
        in_specs=[
            pl.BlockSpec((tm, D_MODEL), lambda i: (i, 0)),
            pl.BlockSpec((D_MODEL, LANE), lambda i: (0, 0)),
        ],
        out_specs=[
            pl.BlockSpec((tm, LANE), lambda i: (i, 0)),
            pl.BlockSpec((8, LANE), lambda i: (0, 0)),
        ],
        out_shape=[jax.ShapeDtypeStruct((T, LANE), F32), jax.ShapeDtypeStruct((8, LANE), F32)],
        scratch_shapes=[pltpu.VMEM((1, LANE), F32)],
        compiler_params=_cparams(("arbitrary",)),
        name="moe_router",
    )(h2, w_router_pad)


def _row_copy(src_hbm, src_row, dst_vmem, dst_row, sem):
    return pltpu.make_async_copy(src_hbm.at[pl.ds(src_row, 1), :], dst_vmem.at[pl.ds(dst_row, 1), :], sem)


def _expert_kernel(tile_expert_ref, n_tiles_ref, slot_token_ref, dest_ref, h_hbm, w1_ref, w3_ref, w2_ref,
                   y_hbm, rows_ref, hb_ref, stage_ref, gsems, ssems, *, tm, chunk):
    i = pl.program_id(0)
    f = pl.program_id(1)
    nf = pl.num_programs(1)
    n_live = n_tiles_ref[0]
    live = i < n_live
    cur = i % 2
    prev = 1 - cur

    def gather_chunk(tile, part, dst):
        for u in range(chunk):
            row = part * chunk + u
            _row_copy(h_hbm, slot_token_ref[tile * tm + row], rows_ref.at[dst], row, gsems.at[dst]).start()

    def scatter_copy(src, row, dest_row):
        return pltpu.make_async_copy(stage_ref.at[src, pl.ds(row, 1), :], y_hbm.at[pl.ds(dest_row, 1), :],
                                     ssems.at[src])

    def scatter_chunk(part):
        for u in range(chunk):
            row = part * chunk + u
            scatter_copy(prev, row, dest_ref[i * tm + row]).start()

    @pl.when((i == 0) & (f == 0))
    def _():
        stage_ref[1] = jnp.zeros(stage_ref.shape[1:], stage_ref.dtype)

        def issue(part, carry):
            gather_chunk(0, part, 0)
            return carry

        lax.fori_loop(0, nf, issue, 0)

    @pl.when((f == 0) & (i <= n_live))
    def _():
        def wait(r, carry):
            _row_copy(h_hbm, 0, rows_ref.at[cur], r, gsems.at[cur]).wait()
            return carry

        lax.fori_loop(0, tm, wait, 0, unroll=chunk)

    @pl.when((f == 0) & (i >= 1) & (i <= n_live + 1))
    def _():
        def wait(r, carry):
            scatter_copy(cur, r, 0).wait()
            return carry

        lax.fori_loop(0, tm, wait, 0, unroll=chunk)

    @pl.when(live & (f == 0))
    def _():
        hb_ref[...] = rows_ref[cur].astype(BF16)
        stage_ref[cur] = jnp.zeros(stage_ref.shape[1:], stage_ref.dtype)

    @pl.when(live)
    def _():
        gather_chunk(i + 1, f, prev)
        scatter_chunk(f)
        h = hb_ref[...]
        act = (_silu(_dot(h, w1_ref[0])) * _dot(h, w3_ref[0])).astype(BF16)
        stage_ref[cur] += _dot(act, w2_ref[0])

    @pl.when(i == n_live)
    def _():
        scatter_chunk(f)


def _expert_ffn(h2, tile_expert, n_tiles, slot_token, dest, w1, w3, w2, n_out_rows):
    tm, tf, nf = MOE_TM, MOE_TF, MOE_STEPS
    max_tiles = tile_expert.shape[0]

    def live_tile(i, nt):
        return jnp.minimum(i, nt[0] - 1)

    def f_idx(i, f, nt):
        return jnp.where(i < nt[0], f, nf - 1)

    grid_spec = pltpu.PrefetchScalarGridSpec(
        num_scalar_prefetch=4,
        grid=(max_tiles, nf),
        in_specs=[
            pl.BlockSpec(memory_space=pl.ANY),
            pl.BlockSpec((1, D_MODEL, tf), lambda i, f, te, nt, st, ds: (te[live_tile(i, nt)], 0, f_idx(i, f, nt))),
            pl.BlockSpec((1, D_MODEL, tf), lambda i, f, te, nt, st, ds: (te[live_tile(i, nt)], 0, f_idx(i, f, nt))),
            pl.BlockSpec((1, tf, D_MODEL), lambda i, f, te, nt, st, ds: (te[live_tile(i, nt)], f_idx(i, f, nt), 0)),
        ],
        out_specs=pl.BlockSpec(memory_space=pl.ANY),
        scratch_shapes=[
            pltpu.VMEM((2, tm, D_MODEL), F32),
            pltpu.VMEM((tm, D_MODEL), BF16),
            pltpu.VMEM((2, tm, D_MODEL), F32),
            pltpu.SemaphoreType.DMA((2,)),
            pltpu.SemaphoreType.DMA((2,)),
        ],
    )
    return pl.pallas_call(
        functools.partial(_expert_kernel, tm=tm, chunk=MOE_CHUNK),
        grid_spec=grid_spec,
        out_shape=jax.ShapeDtypeStruct((n_out_rows, D_MODEL), F32),
        compiler_params=_cparams(("arbitrary", "arbitrary")),
        name="moe_experts",
    )(tile_expert, n_tiles, slot_token, dest, h2, w1, w3, w2)


def _combine_kernel(y1_ref, y2_ref, x_ref, info_ref, mod_ref, g_ref, b_ref, o_ref):
    info = info_ref[...]
    y = info[:, INFO_W:INFO_W + 1] * y1_ref[...] + info[:, INFO_W + 1:INFO_W + 2] * y2_ref[...]
    z = DN_ALPHA * x_ref[...] + mod_ref[0, 5:6, :] * y
    o_ref[...] = _layer_norm(z, g_ref[...], b_ref[...])


def _moe_combine(y_tok, x2, info, mod_l, ln_g, ln_b, seq):
    T = x2.shape[0]
    tm = 512
    per_b = seq // tm
    row = pl.BlockSpec((tm, D_MODEL), lambda i: (i, 0))
    vec = pl.BlockSpec((1, D_MODEL), lambda i: (0, 0))
    return pl.pallas_call(
        _combine_kernel,
        grid=(T // tm,),
        in_specs=[
            row,
            pl.BlockSpec((tm, D_MODEL), lambda i: (T // tm + i, 0)),
            row,
            pl.BlockSpec((tm, LANE), lambda i: (i, 0)),
            pl.BlockSpec((1, N_MOD, D_MODEL), lambda i: (i // per_b, 0, 0)),
            vec, vec,
        ],
        out_specs=row,
        out_shape=jax.ShapeDtypeStruct((T, D_MODEL), F32),
        compiler_params=_cparams(("parallel",)),
        name="moe_combine_ln",
    )(y_tok, y_tok, x2, info, mod_l, ln_g.reshape(1, D_MODEL), ln_b.reshape(1, D_MODEL))


def _moe_sublayer(x2, h2, mod_l, w_router, w1, w3, w2, ln_g, ln_b, seq):
    T = x2.shape[0]
    tm = MOE_TM
    wr_pad = jnp.zeros((D_MODEL, LANE), BF16).at[:, :N_EXPERTS].set(w_router.astype(BF16))
    info, cnt = _router(h2, wr_pad)
    experts = info[:, INFO_I:INFO_I + TOP_K].astype(jnp.int32)
    ranks = info[:, INFO_R:INFO_R + TOP_K].astype(jnp.int32)
    counts = cnt[0, :N_EXPERTS].astype(jnp.int32)
    tiles_e = (counts + tm - 1) // tm
    tile_end = jnp.cumsum(tiles_e)
    tile_start = tile_end - tiles_e
    max_tiles = (T * TOP_K + N_EXPERTS * (tm - 1)) // tm + 2
    slots = (tile_start * tm)[experts] + ranks
    tile_ids = jnp.arange(max_tiles, dtype=jnp.int32)
    tile_expert = jnp.minimum(jnp.sum(tile_ids[:, None] >= tile_end[None, :], axis=1), N_EXPERTS - 1).astype(jnp.int32)
    n_tiles = tile_end[-1:].astype(jnp.int32)
    token_ids = jnp.broadcast_to(jnp.arange(T, dtype=jnp.int32)[:, None], (T, TOP_K))
    n_slots = max_tiles * tm
    spare = TOP_K * T + jnp.arange(n_slots, dtype=jnp.int32) % tm
    out_rows = jnp.arange(TOP_K, dtype=jnp.int32)[None, :] * T + token_ids
    slot_dest = spare.at[slots.reshape(-1)].set(out_rows.reshape(-1))
    slot_token = slot_dest % T
    dest = jnp.concatenate([spare[:tm], slot_dest[:n_slots - tm]])
    y_tok = _expert_ffn(h2, tile_expert, n_tiles, slot_token, dest, w1, w3, w2, TOP_K * T + tm)
    return _moe_combine(y_tok, x2, info, mod_l, ln_g, ln_b, seq)


def _pack_w_in(w_in_l):
    low = jnp.zeros((D_MODEL, N_PROJ - U_LOW * LANE), w_in_l.dtype).at[:, :GLA_RANK].set(w_in_l[:, _REF_LOW0:_REF_G0])
    a_cols = [w_in_l[:, part * A_WIDTH + g * W4:part * A_WIDTH + (g + 1) * W4]
              for g in range(len(DIL_GROUPS)) for part in range(3)]
    return jnp.concatenate(a_cols + [w_in_l[:, 3 * A_WIDTH:_REF_LOW0], w_in_l[:, _REF_G0:], low], axis=1).astype(BF16)


def _pack_w_up(w_up_l):
    pad = jnp.zeros((LANE, GLA_DK), F32).at[:GLA_RANK].set(w_up_l)
    hi = pad.astype(BF16)
    lo = (pad - hi.astype(F32)).astype(BF16)
    return hi, lo


def kernel(x, c, w_ada, b_ada, ln_g, ln_b, w_in, w_alpha_up, b_alpha, gla_norm_g, b_merge, w_branch_a,
           w_branch_b, w_out, ffn_w1, ffn_w3, ffn_w2, w_router, moe_w1, moe_w3, moe_w2):
    batch, seq, _ = x.shape
    T = batch * seq
    x2 = x.reshape(T, D_MODEL)
    mod = _adaln_mod(c, w_ada, b_ada)
    cast_block = {"w1": (1, D_MODEL, MXU_DIM), "w3": (1, D_MODEL, MXU_DIM), "w2": (1, MXU_DIM, D_MODEL)}
    to_cast = [(l, name, w[l // 2]) for l in range(DEPTH) if l % 2 == 1
               for name, w in (("w1", moe_w1), ("w2", moe_w2), ("w3", moe_w3))]
    moe_bf16 = {}

    def next_side(before_layer):
        if to_cast and to_cast[0][0] >= before_layer:
            layer, name, w = to_cast.pop(0)
            return (layer, name), (w, cast_block[name])
        return None, None

    for l in range(DEPTH):
        mod_l = mod[l]
        key, side = next_side(l)
        proj, cast = _in_projection(x2, mod_l, _pack_w_in(w_in[l]), seq, side)
        if key is not None:
            moe_bf16[key] = cast
        groups = [_dilated_group(proj, batch, seq, g) for g in range(len(DIL_GROUPS))]
        w_up_hi, w_up_lo = _pack_w_up(w_alpha_up[l])
        y_b = _gla_branch(proj, batch, seq, w_up_hi, w_up_lo, b_alpha[l], gla_norm_g[l])
        mixed = _branch_mix([g[0] for g in groups], [g[1] for g in groups], y_b, proj, b_merge[l],
                            w_branch_a[l].astype(BF16), w_branch_b[l].astype(BF16))
        moe_layer = l % 2 == 1
        x2, h2 = _out_projection(mixed, x2, mod_l, w_out[l].astype(BF16), ln_g[l, 0], ln_b[l, 0], seq, moe_layer)
        e = l // 2
        if moe_layer:
            to_cast = [item for item in to_cast if item[0] != l]
            w1, w3, w2 = (moe_bf16[(l, name)] if (l, name) in moe_bf16 else w[e].astype(BF16)
                          for name, w in (("w1", moe_w1), ("w3", moe_w3), ("w2", moe_w2)))
            x2 = _moe_sublayer(x2, h2, mod_l, w_router[e], w1, w3, w2, ln_g[l, 1], ln_b[l, 1], seq)
        else:
            key, side = next_side(l + 1)
            x2, cast = _dense_ffn(x2, mod_l, ffn_w1[e].astype(BF16), ffn_w3[e].astype(BF16), ffn_w2[e].astype(BF16),
                                  ln_g[l, 1], ln_b[l, 1], seq, side)
            if key is not None:
                moe_bf16[key] = cast
    return x2.reshape(batch, seq, D_MODEL)
```

```python
import functools

import jax
import jax.numpy as jnp
from jax import lax
from jax.experimental import pallas as pl
from jax.experimental.pallas import tpu as pltpu

F32 = jnp.float32
BF16 = jnp.bfloat16

D_MODEL = 2048
DEPTH = 2
HEAD_DIM = 128
DIL_GROUPS = ((128, 1), (512, 4), (2048, 16))
HEADS_PER_GROUP = 4
A_HEADS = HEADS_PER_GROUP * len(DIL_GROUPS)
A_WIDTH = A_HEADS * HEAD_DIM
A_OUT = HEADS_PER_GROUP * HEAD_DIM
ALIBI_MAX_EXP = 8.0
GLA_HEADS = 4
GLA_DK = D_MODEL // 2
GLA_DV = D_MODEL
GLA_HK = GLA_DK // GLA_HEADS
GLA_HV = GLA_DV // GLA_HEADS
GLA_RANK = 16
GLA_TAU = 16.0
GLA_CHUNK = 64
N_BRANCH = 2
FFN_DENSE = 5632
N_EXPERTS = 8
TOP_K = 2
FFN_EXPERT = 7168
N_MOD = 6
LN_EPS = 1e-5
NORM_EPS = 1e-6
DN_ALPHA = (2 * DEPTH) ** 0.25

LANE = 128
MXU_DIM = 256
VMEM_LIMIT_MB = 56

U_GROUP = 12
U_A_END = 36
U_BQ, U_BK, U_BV, U_BR = 36, 44, 52, 68
U_GA, U_GB = 84, 100
U_LOW = 116
N_UNITS = 120
N_PROJ = N_UNITS * LANE
BAND = 128
W4 = HEADS_PER_GROUP * HEAD_DIM
PERM_TILE = 1024

_REF_LOW0 = 3 * A_WIDTH + 2 * GLA_DK + 2 * GLA_DV
_REF_G0 = _REF_LOW0 + GLA_RANK


def _cparams(semantics, vmem_mb=VMEM_LIMIT_MB):
    return pltpu.CompilerParams(dimension_semantics=semantics, vmem_limit_bytes=vmem_mb << 20)


def _dot(a, b):
    return jnp.dot(a, b, preferred_element_type=F32)


def _dot_nt(a, b):
    return lax.dot_general(a, b, (((1,), (1,)), ((), ())), preferred_element_type=F32)


def _dot_tn(a, b):
    return lax.dot_general(a, b, (((0,), (0,)), ((), ())), preferred_element_type=F32)


def _sigmoid(x):
    return 1.0 / (1.0 + jnp.exp(-x))


def _silu(x):
    return x * _sigmoid(x)


def _layer_norm(z, g, b):
    mu = jnp.mean(z, axis=-1, keepdims=True)
    zc = z - mu
    var = jnp.mean(zc * zc, axis=-1, keepdims=True)
    return zc * lax.rsqrt(var + LN_EPS) * g + b


def _mod_kernel(c_ref, w_ref, b_ref, o_ref):
    s = _silu(c_ref[...]).astype(BF16)
    o_ref[0] = _dot(s, w_ref[0].astype(BF16)) + b_ref[0]


def _adaln_mod(c, w_ada, b_ada):
    B = c.shape[0]
    rows = 8
    tn = 1024
    c_pad = jnp.zeros((rows, D_MODEL), F32).at[:B].set(c)
    n_out = N_MOD * D_MODEL
    out = pl.pallas_call(
        _mod_kernel,
        grid=(DEPTH, n_out // tn),
        in_specs=[
            pl.BlockSpec((rows, D_MODEL), lambda l, j: (0, 0)),
            pl.BlockSpec((1, D_MODEL, tn), lambda l, j: (l, 0, j)),
            pl.BlockSpec((1, 1, tn), lambda l, j: (l, 0, j)),
        ],
        out_specs=pl.BlockSpec((1, rows, tn), lambda l, j: (l, 0, j)),
        out_shape=jax.ShapeDtypeStruct((DEPTH, rows, n_out), F32),
        compiler_params=_cparams(("arbitrary", "arbitrary")),
        name="adaln_mod",
    )(c_pad, w_ada, b_ada.reshape(DEPTH, 1, n_out))
    return out[:, :B].reshape(DEPTH, B, N_MOD, D_MODEL)


def _unit_dilation(unit):
    return DIL_GROUPS[unit // U_GROUP][1] if unit < U_A_END else 1


def _side_cast_specs(src, block, steps_inner):
    n_e, n_r, n_c = src.shape
    _, br, bc = block
    per_e = (n_r // br) * (n_c // bc)
    last = n_e * per_e - 1

    def index(i, j):
        s = jnp.minimum(i * steps_inner + j, last)
        rem = s % per_e
        return s // per_e, rem // (n_c // bc), rem % (n_c // bc)

    return pl.BlockSpec(block, index), last + 1


def _inproj_kernel(*refs, tm, tn, side):
    if side:
        x_ref, mod_ref, w_ref, side_ref, o_ref, side_out_ref, h_ref, slab_ref = refs
        side_out_ref[...] = side_ref[...].astype(side_out_ref.dtype)
    else:
        x_ref, mod_ref, w_ref, o_ref, h_ref, slab_ref = refs
    j = pl.program_id(1)

    @pl.when(j == 0)
    def _():
        shift = mod_ref[0, 0:1, :]
        scale = mod_ref[0, 1:2, :]
        h_ref[...] = (x_ref[...] * (1.0 + scale) + shift).astype(BF16)

    acc = _dot(h_ref[...], w_ref[...])
    parts = tn // W4
    tile_dils = [[_unit_dilation((jj * tn + p * W4) // LANE) for p in range(parts)] for jj in range(N_PROJ // tn)]
    perm_tiles = [jj for jj, ds in enumerate(tile_dils) if any(d > 1 for d in ds)]
    is_perm = functools.reduce(jnp.logical_or, [j == jj for jj in perm_tiles])

    @pl.when(jnp.logical_not(is_perm))
    def _():
        o_ref[...] = acc.astype(o_ref.dtype)

    for jj in perm_tiles:
        @pl.when(j == jj)
        def _(jj=jj):
            for p, dil in enumerate(tile_dils[jj]):
                c0 = p * W4
                if dil == 1:
                    o_ref[:, c0:c0 + W4] = acc[:, c0:c0 + W4].astype(o_ref.dtype)
                    continue
                n = tm // dil
                for s in range(W4 // LANE):
                    slab_ref[s] = acc[:, c0 + s * LANE:c0 + (s + 1) * LANE]
                for s in range(W4 // LANE):
                    for r in range(dil):
                        o_ref[r * n:(r + 1) * n, c0 + s * LANE:c0 + (s + 1) * LANE] = (
                            slab_ref[s, pl.ds(r, n, stride=dil), :].astype(o_ref.dtype))


def _in_projection(x2, mod_l, w_cat, seq, side=None):
    T = x2.shape[0]
    tm = PERM_TILE
    tn = 1024
    per_b = seq // tm
    grid = (T // tm, N_PROJ // tn)
    in_specs = [
        pl.BlockSpec((tm, D_MODEL), lambda i, j: (i, 0)),
        pl.BlockSpec((1, N_MOD, D_MODEL), lambda i, j: (i // per_b, 0, 0)),
        pl.BlockSpec((D_MODEL, tn), lambda i, j: (0, j)),
    ]
    out_specs = [pl.BlockSpec((tm, tn), lambda i, j: (i, j))]
    out_shape = [jax.ShapeDtypeStruct((T, N_PROJ), BF16)]
    args = [x2, mod_l, w_cat]
    if side is not None:
        src, block = side
        spec, n_blocks = _side_cast_specs(src, block, grid[1])
        assert n_blocks <= grid[0] * grid[1]
        in_specs.append(spec)
        out_specs.append(spec)
        out_shape.append(jax.ShapeDtypeStruct(src.shape, BF16))
        args.append(src)
    outs = pl.pallas_call(
        functools.partial(_inproj_kernel, tm=tm, tn=tn, side=side is not None),
        grid=grid,
        in_specs=in_specs,
        out_specs=out_specs,
        out_shape=out_shape,
        scratch_shapes=[pltpu.VMEM((tm, D_MODEL), BF16), pltpu.VMEM((W4 // LANE, tm, LANE), F32)],
        compiler_params=_cparams(("arbitrary", "arbitrary")),
        name="in_projection",
    )(*args)
    return (outs[0], outs[1]) if side is not None else (outs[0], None)


LSE_LANES = LANE // HEADS_PER_GROUP


def _attn_kernel(*refs, c, nt, bias_scale, has_prev):
    if has_prev:
        q_ref, kp_ref, k_ref, vp_ref, v_ref, o_ref, lse_ref = refs
        first_run = pl.program_id(1) == 0
    else:
        q_ref, k_ref, v_ref, o_ref, lse_ref = refs
    qi = lax.broadcasted_iota(jnp.int32, (BAND, BAND), 0)
    ki = lax.broadcasted_iota(jnp.int32, (BAND, BAND), 1)
    valid_prev = ki >= qi
    valid_cur = ki <= qi
    delta_prev = (qi + BAND - ki).astype(F32)
    delta_cur = (qi - ki).astype(F32)
    neg_inf = jnp.float32(-jnp.inf)
    sm_scale = HEAD_DIM ** -0.5

    def rows(sb):
        if c >= BAND:
            per = c // BAND
            return sb // per, slice((sb % per) * BAND, (sb % per + 1) * BAND)
        per = BAND // c
        return slice(sb * per, (sb + 1) * per), slice(None)

    def load(ref, sb, cs):
        t, r = rows(sb)
        return ref[t, r, cs].reshape(BAND, cs.stop - cs.start)

    def store(ref, sb, cs, val):
        t, r = rows(sb)
        if c < BAND:
            val = val.reshape(BAND // c, c, cs.stop - cs.start)
        ref[t, r, cs] = val

    heads = range(HEADS_PER_GROUP)
    cols = [slice(hh * HEAD_DIM, (hh + 1) * HEAD_DIM) for hh in heads]
    bias_cur = jnp.concatenate([jnp.where(valid_cur, -bias_scale[hh] * delta_cur, neg_inf) for hh in heads], axis=0)
    bias_prev = jnp.concatenate([jnp.where(valid_prev, -bias_scale[hh] * delta_prev, neg_inf) for hh in heads], axis=0)
    head_rows = [slice(hh * BAND, (hh + 1) * BAND) for hh in heads]
    for sb in range(nt * c // BAND):
        with_prev = sb > 0 or has_prev
        qs = [load(q_ref, sb, cs) for cs in cols]
        v_cur = [load(v_ref, sb, cs) for cs in cols]
        s_cur = jnp.concatenate([_dot_nt(qs[hh], load(k_ref, sb, cols[hh])) for hh in heads], axis=0)
        s_cur = s_cur * sm_scale + bias_cur
        m = jnp.max(s_cur, axis=1, keepdims=True)
        if with_prev:
            if sb > 0:
                k_prev = [load(k_ref, sb - 1, cs) for cs in cols]
                v_prev = [load(v_ref, sb - 1, cs) for cs in cols]
            else:
                k_prev = [kp_ref[0, :, cs] for cs in cols]
                v_prev = [vp_ref[0, :, cs] for cs in cols]
            s_prev = jnp.concatenate([_dot_nt(qs[hh], k_prev[hh]) for hh in heads], axis=0)
            s_prev = s_prev * sm_scale + bias_prev
            if sb == 0:
                s_prev = jnp.where(first_run, neg_inf, s_prev)
            m = jnp.maximum(m, jnp.max(s_prev, axis=1, keepdims=True))
            p_prev = jnp.exp(s_prev - m)
        p_cur = jnp.exp(s_cur - m)
        den = jnp.sum(p_cur, axis=1, keepdims=True)
        if with_prev:
            den = den + jnp.sum(p_prev, axis=1, keepdims=True)
        inv_den = 1.0 / den
        lse = m + jnp.log(den)
        p_cur = p_cur.astype(BF16)
        if with_prev:
            p_prev = p_prev.astype(BF16)
        for hh in heads:
            o = _dot(p_cur[head_rows[hh]], v_cur[hh])
            if with_prev:
                o = o + _dot(p_prev[head_rows[hh]], v_prev[hh])
            store(o_ref, sb, cols[hh], o * inv_den[head_rows[hh]])
            store(lse_ref, sb, slice(hh * LSE_LANES, (hh + 1) * LSE_LANES),
                  jnp.broadcast_to(lse[head_rows[hh]], (BAND, LSE_LANES)))


def _dilated_group(proj, batch, seq, g):
    _, dil = DIL_GROUPS[g]
    T = batch * seq
    tiles = T // PERM_TILE
    per_b = seq // PERM_TILE
    qcol, kcol, vcol = (g * U_GROUP * LANE // W4 + i for i in range(3))
    slopes = [2.0 ** (-ALIBI_MAX_EXP * (g * HEADS_PER_GROUP + h + 1) / A_HEADS) * dil
              for h in range(HEADS_PER_GROUP)]
    if dil == 1:
        c, nt = PERM_TILE, 1
        runs = proj.reshape(tiles, PERM_TILE, N_PROJ)
        bands = proj.reshape(T // BAND, BAND, N_PROJ)
        per_run = PERM_TILE // BAND
        cur = lambda col: pl.BlockSpec((1, c, W4), lambda b, n: (b * per_b + n, 0, col))
        prev = lambda col: pl.BlockSpec(
            (1, BAND, W4), lambda b, n: (jnp.maximum((b * per_b + n) * per_run - 1, 0), 0, col))
        grid = (batch, per_b)
        in_specs = [cur(qcol), prev(kcol), cur(kcol), prev(vcol), cur(vcol)]
        args = (runs, bands, runs, bands, runs)
        out_specs = [pl.BlockSpec((1, c, W4), lambda b, n: (b * per_b + n, 0, 0)),
                     pl.BlockSpec((1, c, LANE), lambda b, n: (b * per_b + n, 0, 0))]
        out_shape = [jax.ShapeDtypeStruct((tiles, c, W4), F32), jax.ShapeDtypeStruct((tiles, c, LANE), F32)]
        semantics = ("parallel", "arbitrary")
    else:
        c, nt = PERM_TILE // dil, per_b
        runs = proj.reshape(tiles, dil, c, N_PROJ)
        cur = lambda col: pl.BlockSpec((nt, None, c, W4), lambda b, r: (b, r, 0, col))
        grid = (batch, dil)
        in_specs = [cur(qcol), cur(kcol), cur(vcol)]
        args = (runs, runs, runs)
        out_specs = [pl.BlockSpec((nt, None, c, W4), lambda b, r: (b, r, 0, 0)),
                     pl.BlockSpec((nt, None, c, LANE), lambda b, r: (b, r, 0, 0))]
        out_shape = [jax.ShapeDtypeStruct((tiles, dil, c, W4), F32),
                     jax.ShapeDtypeStruct((tiles, dil, c, LANE), F32)]
        semantics = ("parallel", "parallel")
    o, lse = pl.pallas_call(
        functools.partial(_attn_kernel, c=c, nt=nt, bias_scale=slopes, has_prev=dil == 1),
        grid=grid,
        in_specs=in_specs,
        out_specs=out_specs,
        out_shape=out_shape,
        compiler_params=_cparams(semantics),
        name=f"dilated_attn_g{g}",
    )(*args)
    return o.reshape(T, W4), lse.reshape(T, LANE)


GLA_GROUP = 256


GLA_HEADS_PER_STEP = 2


def _gla_kernel(*refs, tb, hp):
    q_ref, k_ref = refs[:2]
    v_refs, r_refs = refs[2:2 + hp], refs[2 + hp:2 + 2 * hp]
    low_ref, whi_ref, wlo_ref, balpha_ref, gnorm_ref, y_ref, state_ref = refs[2 + 2 * hp:]

    @pl.when(pl.program_id(2) == 0)
    def _():
        state_ref[...] = jnp.zeros_like(state_ref)

    C = GLA_CHUNK
    G = GLA_GROUP
    kw = hp * GLA_HK
    low = low_ref[...]
    z = _dot(low, whi_ref[...]) + _dot(low, wlo_ref[...]) + balpha_ref[...]
    log_a = (jnp.minimum(z, 0.0) - jnp.log(1.0 + jnp.exp(-jnp.abs(z)))) * (1.0 / GLA_TAU)

    ri = lax.broadcasted_iota(jnp.int32, (G, G), 0)
    ci = lax.broadcasted_iota(jnp.int32, (G, G), 1)
    same_chunk = (ri // C) == (ci // C)
    causal = same_chunk & (ci <= ri)
    tril = causal.astype(BF16)
    sel_r = lax.broadcasted_iota(jnp.int32, (G, (G // C) * LANE), 0)
    sel_c = lax.broadcasted_iota(jnp.int32, (G, (G // C) * LANE), 1)
    chunk_sel = ((sel_r // C) == (sel_c // LANE)).astype(BF16)

    groups = [slice(g * G, (g + 1) * G) for g in range(tb // G)]
    head_cols = [slice(h * GLA_HK, (h + 1) * GLA_HK) for h in range(hp)]
    gcs, decay_cols = [], {}
    for gi, rows in enumerate(groups):
        la = log_a[rows]
        la_hi = la.astype(BF16)
        la_lo = (la - la_hi.astype(F32)).astype(BF16)
        gcs.append(_dot(tril, la_hi) + _dot(tril, la_lo))
        for h, hc in enumerate(head_cols):
            decay_cols[gi, h] = jnp.exp(_dot_tn(la_hi[:, hc], chunk_sel) + _dot_tn(la_lo[:, hc], chunk_sel))
    gc = jnp.concatenate(gcs, axis=0)
    chunks = [slice(c * C, (c + 1) * C) for c in range(tb // C)]
    g_mid = jnp.concatenate([jnp.broadcast_to(gc[cr][C // 2 - 1:C // 2], (C, kw)) for cr in chunks], axis=0)
    g_end = jnp.concatenate([jnp.broadcast_to(gc[cr][C - 1:C], (C, kw)) for cr in chunks], axis=0)
    q_all = q_ref[...].astype(F32) * (GLA_HK ** -0.5)
    k_all = k_ref[...].astype(F32)
    qe = (q_all * jnp.exp(gc - g_mid)).astype(BF16)
    ke = (k_all * jnp.exp(g_mid - gc)).astype(BF16)
    q_in = (q_all * jnp.exp(gc)).astype(BF16)
    k_out = (k_all * jnp.exp(g_end - gc)).astype(BF16)

    for h, hc in enumerate(head_cols):
        v_ref = v_refs[h]
        state = state_ref[h]
        outs = []
        for gi, rows in enumerate(groups):
            o_inter = []
            for c in range(G // C):
                cr = slice(gi * G + c * C, gi * G + (c + 1) * C)
                update = _dot_tn(k_out[cr, hc], v_ref[cr, :])
                o_inter.append(_dot(q_in[cr, hc], state.astype(BF16)))
                decay = decay_cols[gi, h][:, c * LANE:(c + 1) * LANE]
                state = jnp.concatenate(
                    [state[:, j * LANE:(j + 1) * LANE] * decay for j in range(GLA_HV // LANE)], axis=1) + update
            att = jnp.where(causal, _dot_nt(qe[rows, hc], ke[rows, hc]), 0.0).astype(BF16)
            outs.append(_dot(att, v_ref[rows, :]) + jnp.concatenate(o_inter, axis=0))
        state_ref[h] = state
        o = jnp.concatenate(outs, axis=0)
        ms = jnp.mean(o * o, axis=-1, keepdims=True)
        o = o * lax.rsqrt(ms + NORM_EPS) * gnorm_ref[...]
        y_ref[:, h * GLA_HV:(h + 1) * GLA_HV] = (_silu(r_refs[h][...].astype(F32)) * o).astype(y_ref.dtype)


def _gla_branch(proj, batch, seq, w_up_hi, w_up_lo, b_alpha, gnorm):
    T = batch * seq
    tb = 512
    hp = GLA_HEADS_PER_STEP
    kw = hp * GLA_HK
    per_b = seq // tb

    def rows(b, n):
        return b * per_b + n

    def per_head(unit, h):
        return pl.BlockSpec((tb, GLA_HV), lambda b, hb, n: (rows(b, n), unit * LANE // GLA_HV + hb * hp + h))

    in_specs = [
        pl.BlockSpec((tb, kw), lambda b, hb, n: (rows(b, n), U_BQ * LANE // kw + hb)),
        pl.BlockSpec((tb, kw), lambda b, hb, n: (rows(b, n), U_BK * LANE // kw + hb)),
        *[per_head(U_BV, h) for h in range(hp)],
        *[per_head(U_BR, h) for h in range(hp)],
        pl.BlockSpec((tb, LANE), lambda b, hb, n: (rows(b, n), U_LOW)),
        pl.BlockSpec((LANE, kw), lambda b, hb, n: (0, hb)),
        pl.BlockSpec((LANE, kw), lambda b, hb, n: (0, hb)),
        pl.BlockSpec((1, kw), lambda b, hb, n: (0, hb)),
        pl.BlockSpec((1, GLA_HV), lambda b, hb, n: (0, 0)),
    ]
    return pl.pallas_call(
        functools.partial(_gla_kernel, tb=tb, hp=hp),
        grid=(batch, GLA_HEADS // hp, per_b),
        in_specs=in_specs,
        out_specs=pl.BlockSpec((tb, hp * GLA_HV), lambda b, hb, n: (rows(b, n), hb)),
        out_shape=jax.ShapeDtypeStruct((T, GLA_DV), BF16),
        scratch_shapes=[pltpu.VMEM((hp, GLA_HK, GLA_HV), F32)],
        compiler_params=_cparams(("parallel", "parallel", "arbitrary")),
        name="gla_branch",
    )(*([proj] * (3 + 2 * hp)), w_up_hi, w_up_lo, b_alpha.reshape(1, GLA_DK), gnorm.reshape(1, GLA_HV))


def _branch_kernel(o0_ref, o1_ref, o2_ref, l0_ref, l1_ref, l2_ref, yb_ref, ga_ref, gb_ref, bm_ref,
                   wa_ref, wb_ref, out_ref, ya_ref, oslab_ref, lslab_ref, *, tm):
    @pl.when(pl.program_id(1) == 0)
    def _():
        for gi, (o_ref, l_ref) in enumerate(((o1_ref, l1_ref), (o2_ref, l2_ref))):
            dil = DIL_GROUPS[gi + 1][1]
            n = tm // dil
            for r in range(dil):
                src = slice(r * n, (r + 1) * n)
                dst = pl.ds(r, n, stride=dil)
                for h in range(HEADS_PER_GROUP):
                    oslab_ref[gi, h, dst, :] = o_ref[src, h * HEAD_DIM:(h + 1) * HEAD_DIM]
                lslab_ref[gi, dst, :] = l_ref[src, :]
        l0, l1, l2 = l0_ref[...], lslab_ref[0], lslab_ref[1]
        m = jnp.maximum(jnp.maximum(l0, l1), l2)
        e0, e1, e2 = jnp.exp(l0 - m), jnp.exp(l1 - m), jnp.exp(l2 - m)
        inv = 1.0 / (e0 + e1 + e2)
        w0, w1, w2 = e0 * inv, e1 * inv, e2 * inv
        for h in range(HEADS_PER_GROUP):
            cs = slice(h * HEAD_DIM, (h + 1) * HEAD_DIM)
            lane = slice(h * LSE_LANES, h * LSE_LANES + 1)
            ya = o0_ref[:, cs] * w0[:, lane] + oslab_ref[0, h] * w1[:, lane] + oslab_ref[1, h] * w2[:, lane]
            ya_ref[:, cs] = ya.astype(BF16)

    pa = _dot(ya_ref[...], wa_ref[...])
    pb = _dot(yb_ref[...], wb_ref[...])
    gate_a = _sigmoid(ga_ref[...].astype(F32) + bm_ref[0:1, :])
    gate_b = _sigmoid(gb_ref[...].astype(F32) + bm_ref[1:2, :])
    out_ref[...] = (gate_a * pa + gate_b * pb).astype(out_ref.dtype)


def _branch_mix(outs, lses, y_b, proj, b_merge, wa, wb):
    T = y_b.shape[0]
    tm, tn = PERM_TILE, 512
    row = lambda i, j: (i, 0)
    n_dilated = len(DIL_GROUPS) - 1
    return pl.pallas_call(
        functools.partial(_branch_kernel, tm=tm),
        grid=(T // tm, D_MODEL // tn),
        in_specs=[pl.BlockSpec((tm, A_OUT), row)] * 3 + [pl.BlockSpec((tm, LANE), row)] * 3 + [
            pl.BlockSpec((tm, GLA_DV), row),
            pl.BlockSpec((tm, tn), lambda i, j: (i, U_GA * LANE // tn + j)),
            pl.BlockSpec((tm, tn), lambda i, j: (i, U_GB * LANE // tn + j)),
            pl.BlockSpec((N_BRANCH, tn), lambda i, j: (0, j)),
            pl.BlockSpec((A_OUT, tn), lambda i, j: (0, j)),
            pl.BlockSpec((GLA_DV, tn), lambda i, j: (0, j)),
        ],
        out_specs=pl.BlockSpec((tm, tn), lambda i, j: (i, j)),
        out_shape=jax.ShapeDtypeStruct((T, D_MODEL), BF16),
        scratch_shapes=[
            pltpu.VMEM((tm, A_OUT), BF16),
            pltpu.VMEM((n_dilated, HEADS_PER_GROUP, tm, HEAD_DIM), F32),
            pltpu.VMEM((n_dilated, tm, LANE), F32),
        ],
        compiler_params=_cparams(("parallel", "arbitrary")),
        name="branch_mix",
    )(*outs, *lses, y_b, proj, proj, b_merge, wa, wb)


def _outproj_kernel(mixed_ref, x_ref, mod_ref, w_ref, g_ref, b_ref, *out_refs, emit_h, tm, parts):
    gate = mod_ref[0, 2:3, :]
    for p in range(parts):
        rs = slice(p * (tm // parts), (p + 1) * (tm // parts))
        y = _dot(mixed_ref[rs, :], w_ref[...])
        xn = _layer_norm(DN_ALPHA * x_ref[rs, :] + gate * y, g_ref[...], b_ref[...])
        out_refs[0][rs, :] = xn
        if emit_h:
            out_refs[1][rs, :] = xn * (1.0 + mod_ref[0, 4:5, :]) + mod_ref[0, 3:4, :]


def _out_projection(mixed, x2, mod_l, w_out, ln_g, ln_b, seq, emit_h):
    T = x2.shape[0]
    tm, parts = 512, 2
    per_b = seq // tm
    row = pl.BlockSpec((tm, D_MODEL), lambda i: (i, 0))
    vec = pl.BlockSpec((1, D_MODEL), lambda i: (0, 0))
    n_out = 2 if emit_h else 1
    outs = pl.pallas_call(
        functools.partial(_outproj_kernel, emit_h=emit_h, tm=tm, parts=parts),
        grid=(T // tm,),
        in_specs=[
            row, row,
            pl.BlockSpec((1, N_MOD, D_MODEL), lambda i: (i // per_b, 0, 0)),
            pl.BlockSpec((D_MODEL, D_MODEL), lambda i: (0, 0), pipeline_mode=pl.Buffered(1)),
            vec, vec,
        ],
        out_specs=[row] * n_out,
        out_shape=[jax.ShapeDtypeStruct((T, D_MODEL), F32)] * n_out,
        compiler_params=_cparams(("parallel",)),
        name="out_projection_ln",
    )(mixed, x2, mod_l, w_out, ln_g.reshape(1, D_MODEL), ln_b.reshape(1, D_MODEL))
    return outs if emit_h else (outs[0], None)


def _ffn_kernel(*refs, side):
    if side:
        x_ref, mod_ref, w1_ref, w3_ref, w2_ref, g_ref, b_ref, side_ref, o_ref, side_out_ref, h_ref, acc_ref = refs
        side_out_ref[...] = side_ref[...].astype(side_out_ref.dtype)
    else:
        x_ref, mod_ref, w1_ref, w3_ref, w2_ref, g_ref, b_ref, o_ref, h_ref, acc_ref = refs
    f = pl.program_id(1)

    @pl.when(f == 0)
    def _():
        h_ref[...] = (x_ref[...] * (1.0 + mod_ref[0, 4:5, :]) + mod_ref[0, 3:4, :]).astype(BF16)
        acc_ref[...] = jnp.zeros_like(acc_ref)

    h = h_ref[...]
    act = (_silu(_dot(h, w1_ref[...])) * _dot(h, w3_ref[...])).astype(BF16)
    acc_ref[...] += _dot(act, w2_ref[...])

    @pl.when(f == pl.num_programs(1) - 1)
    def _():
        z = DN_ALPHA * x_ref[...] + mod_ref[0, 5:6, :] * acc_ref[...]
        o_ref[...] = _layer_norm(z, g_ref[...], b_ref[...])


def _dense_ffn(x2, mod_l, w1, w3, w2, ln_g, ln_b, seq, side=None):
    T = x2.shape[0]
    tm, tf = 512, 512
    per_b = seq // tm
    grid = (T // tm, FFN_DENSE // tf)
    row = pl.BlockSpec((tm, D_MODEL), lambda i, f: (i, 0))
    vec = pl.BlockSpec((1, D_MODEL), lambda i, f: (0, 0))
    in_specs = [
        row,
        pl.BlockSpec((1, N_MOD, D_MODEL), lambda i, f: (i // per_b, 0, 0)),
        pl.BlockSpec((D_MODEL, tf), lambda i, f: (0, f)),
        pl.BlockSpec((D_MODEL, tf), lambda i, f: (0, f)),
        pl.BlockSpec((tf, D_MODEL), lambda i, f: (f, 0)),
        vec, vec,
    ]
    out_specs = [row]
    out_shape = [jax.ShapeDtypeStruct((T, D_MODEL), F32)]
    args = [x2, mod_l, w1, w3, w2, ln_g.reshape(1, D_MODEL), ln_b.reshape(1, D_MODEL)]
    if side is not None:
        src, block = side
        spec, n_blocks = _side_cast_specs(src, block, grid[1])
        assert n_blocks <= grid[0] * grid[1]
        in_specs.append(spec)
        out_specs.append(spec)
        out_shape.append(jax.ShapeDtypeStruct(src.shape, BF16))
        args.append(src)
    outs = pl.pallas_call(
        functools.partial(_ffn_kernel, side=side is not None),
        grid=grid,
        in_specs=in_specs,
        out_specs=out_specs,
        out_shape=out_shape,
        scratch_shapes=[pltpu.VMEM((tm, D_MODEL), BF16), pltpu.VMEM((tm, D_MODEL), F32)],
        compiler_params=_cparams(("arbitrary", "arbitrary")),
        name="dense_ffn_ln",
    )(*args)
    return (outs[0], outs[1]) if side is not None else (outs[0], None)


MOE_TF = 1024
MOE_STEPS = FFN_EXPERT // MOE_TF
MOE_CHUNK = 80
MOE_TM = MOE_STEPS * MOE_CHUNK
MOE_SHORT_ROWS = 288
INFO_I, INFO_W, INFO_R = 0, 2, 4


def _router_kernel(h_ref, wr_ref, info_ref, cnt_ref, carry_ref, *, tm):
    @pl.when(pl.program_id(0) == 0)
    def _():
        carry_ref[...] = jnp.zeros_like(carry_ref)

    logits = _dot(h_ref[...].astype(BF16), wr_ref[...])
    lane = lax.broadcasted_iota(jnp.int32, (tm, LANE), 1).astype(F32)
    neg_inf = jnp.float32(-jnp.inf)
    lg = jnp.where(lane < N_EXPERTS, logits, neg_inf)
    m1 = jnp.max(lg, axis=1, keepdims=True)
    i1 = jnp.min(jnp.where(lg == m1, lane, float(LANE)), axis=1, keepdims=True)
    lg2 = jnp.where(lane == i1, neg_inf, lg)
    m2 = jnp.max(lg2, axis=1, keepdims=True)
    i2 = jnp.min(jnp.where(lg2 == m2, lane, float(LANE)), axis=1, keepdims=True)
    e = jnp.exp(m2 - m1)
    w1 = 1.0 / (1.0 + e)
    w2 = e / (1.0 + e)
    oh1 = lane == i1
    oh2 = lane == i2
    onehot = (oh1 | oh2).astype(BF16)
    ri = lax.broadcasted_iota(jnp.int32, (tm, tm), 0)
    ci = lax.broadcasted_iota(jnp.int32, (tm, tm), 1)
    strict_lower = (ci < ri).astype(BF16)
    rank = _dot(strict_lower, onehot) + carry_ref[...]
    r1 = jnp.sum(jnp.where(oh1, rank, 0.0), axis=1, keepdims=True)
    r2 = jnp.sum(jnp.where(oh2, rank, 0.0), axis=1, keepdims=True)
    carry = carry_ref[...] + jnp.sum(onehot.astype(F32), axis=0, keepdims=True)
    carry_ref[...] = carry
    cnt_ref[...] = jnp.broadcast_to(carry, cnt_ref.shape)
    info = jnp.where(lane == INFO_I, i1, 0.0)
    info = jnp.where(lane == INFO_I + 1, i2, info)
    info = jnp.where(lane == INFO_W, w1, info)
    info = jnp.where(lane == INFO_W + 1, w2, info)
    info = jnp.where(lane == INFO_R, r1, info)
    info = jnp.where(lane == INFO_R + 1, r2, info)
    info_ref[...] = info


def _router(h2, w_router_pad):
    T = h2.shape[0]
    tm = 512
    return pl.pallas_call(
        functools.partial(_router_kernel, tm=tm),
        grid=(T // tm,),
        in_specs=[
            pl.BlockSpec((tm, D_MODEL), lambda i: (i, 0)),
            pl.BlockSpec((D_MODEL, LANE), lambda i: (0, 0)),
        ],
        out_specs=[
            pl.BlockSpec((tm, LANE), lambda i: (i, 0)),
            pl.BlockSpec((8, LANE), lambda i: (0, 0)),
        ],
        out_shape=[jax.ShapeDtypeStruct((T, LANE), F32), jax.ShapeDtypeStruct((8, LANE), F32)],
        scratch_shapes=[pltpu.VMEM((1, LANE), F32)],
        compiler_params=_cparams(("arbitrary",)),
        name="moe_router",
    )(h2, w_router_pad)


def _row_copy(src_hbm, src_row, dst_vmem, dst_row, sem):
    return pltpu.make_async_copy(src_hbm.at[pl.ds(src_row, 1), :], dst_vmem.at[pl.ds(dst_row, 1), :], sem)


def _expert_kernel(tile_expert_ref, n_tiles_ref, slot_token_ref, dest_ref, short_ref, h_hbm, w1_ref, w3_ref, w2_ref,
                   y_hbm, rows_ref, hb_ref, stage_ref, gsems, ssems, *, tm, chunk, short_rows):
    i = pl.program_id(0)
    f = pl.program_id(1)
    nf = pl.num_programs(1)
    n_live = n_tiles_ref[0]
    live = i < n_live
    cur = i % 2
    prev = 1 - cur

    def gather_chunk(tile, part, dst):
        for u in range(chunk):
            row = part * chunk + u
            _row_copy(h_hbm, slot_token_ref[tile * tm + row], rows_ref.at[dst], row, gsems.at[dst]).start()

    def scatter_copy(src, row, dest_row):
        return pltpu.make_async_copy(stage_ref.at[src, pl.ds(row, 1), :], y_hbm.at[pl.ds(dest_row, 1), :],
                                     ssems.at[src])

    def scatter_chunk(part):
        for u in range(chunk):
            row = part * chunk + u
            scatter_copy(prev, row, dest_ref[i * tm + row]).start()

    @pl.when((i == 0) & (f == 0))
    def _():
        stage_ref[1] = jnp.zeros(stage_ref.shape[1:], stage_ref.dtype)

        def issue(part, carry):
            gather_chunk(0, part, 0)
            return carry

        lax.fori_loop(0, nf, issue, 0)

    @pl.when((f == 0) & (i <= n_live))
    def _():
        def wait(r, carry):
            _row_copy(h_hbm, 0, rows_ref.at[cur], r, gsems.at[cur]).wait()
            return carry

        lax.fori_loop(0, tm, wait, 0, unroll=chunk)

    @pl.when((f == 0) & (i >= 1) & (i <= n_live + 1))
    def _():
        def wait(r, carry):
            scatter_copy(cur, r, 0).wait()
            return carry

        lax.fori_loop(0, tm, wait, 0, unroll=chunk)

    @pl.when(live & (f == 0))
    def _():
        hb_ref[...] = rows_ref[cur].astype(BF16)
        stage_ref[cur] = jnp.zeros(stage_ref.shape[1:], stage_ref.dtype)

    def expert_step(n_rows):
        gather_chunk(i + 1, f, prev)
        scatter_chunk(f)
        h = hb_ref[0:n_rows, :]
        act = (_silu(_dot(h, w1_ref[0])) * _dot(h, w3_ref[0])).astype(BF16)
        stage_ref[cur, 0:n_rows, :] += _dot(act, w2_ref[0])

    short = short_ref[i] == 1

    @pl.when(live & jnp.logical_not(short))
    def _():
        expert_step(tm)

    @pl.when(live & short)
    def _():
        expert_step(short_rows)

    @pl.when(i == n_live)
    def _():
        scatter_chunk(f)


def _expert_ffn(h2, tile_expert, n_tiles, slot_token, dest, short, w1, w3, w2, n_out_rows):
    tm, tf, nf = MOE_TM, MOE_TF, MOE_STEPS
    max_tiles = tile_expert.shape[0]

    def live_tile(i, nt):
        return jnp.minimum(i, nt[0] - 1)

    def f_idx(i, f, nt):
        return jnp.where(i < nt[0], f, nf - 1)

    grid_spec = pltpu.PrefetchScalarGridSpec(
        num_scalar_prefetch=5,
        grid=(max_tiles, nf),
        in_specs=[
            pl.BlockSpec(memory_space=pl.ANY),
            pl.BlockSpec((1, D_MODEL, tf), lambda i, f, te, nt, *_: (te[live_tile(i, nt)], 0, f_idx(i, f, nt))),
            pl.BlockSpec((1, D_MODEL, tf), lambda i, f, te, nt, *_: (te[live_tile(i, nt)], 0, f_idx(i, f, nt))),
            pl.BlockSpec((1, tf, D_MODEL), lambda i, f, te, nt, *_: (te[live_tile(i, nt)], f_idx(i, f, nt), 0)),
        ],
        out_specs=pl.BlockSpec(memory_space=pl.ANY),
        scratch_shapes=[
            pltpu.VMEM((2, tm, D_MODEL), F32),
            pltpu.VMEM((tm, D_MODEL), BF16),
            pltpu.VMEM((2, tm, D_MODEL), F32),
            pltpu.SemaphoreType.DMA((2,)),
            pltpu.SemaphoreType.DMA((2,)),
        ],
    )
    return pl.pallas_call(
        functools.partial(_expert_kernel, tm=tm, chunk=MOE_CHUNK, short_rows=MOE_SHORT_ROWS),
        grid_spec=grid_spec,
        out_shape=jax.ShapeDtypeStruct((n_out_rows, D_MODEL), F32),
        compiler_params=_cparams(("arbitrary", "arbitrary")),
        name="moe_experts",
    )(tile_expert, n_tiles, slot_token, dest, short, h2, w1, w3, w2)


def _combine_kernel(y1_ref, y2_ref, x_ref, info_ref, mod_ref, g_ref, b_ref, o_ref):
    info = info_ref[...]
    y = info[:, INFO_W:INFO_W + 1] * y1_ref[...] + info[:, INFO_W + 1:INFO_W + 2] * y2_ref[...]
    z = DN_ALPHA * x_ref[...] + mod_ref[0, 5:6, :] * y
    o_ref[...] = _layer_norm(z, g_ref[...], b_ref[...])


def _moe_combine(y_tok, x2, info, mod_l, ln_g, ln_b, seq):
    T = x2.shape[0]
    tm = 512
    per_b = seq // tm
    row = pl.BlockSpec((tm, D_MODEL), lambda i: (i, 0))
    vec = pl.BlockSpec((1, D_MODEL), lambda i: (0, 0))
    return pl.pallas_call(
        _combine_kernel,
        grid=(T // tm,),
        in_specs=[
            row,
            pl.BlockSpec((tm, D_MODEL), lambda i: (T // tm + i, 0)),
            row,
            pl.BlockSpec((tm, LANE), lambda i: (i, 0)),
            pl.BlockSpec((1, N_MOD, D_MODEL), lambda i: (i // per_b, 0, 0)),
            vec, vec,
        ],
        out_specs=row,
        out_shape=jax.ShapeDtypeStruct((T, D_MODEL), F32),
        compiler_params=_cparams(("parallel",)),
        name="moe_combine_ln",
    )(y_tok, y_tok, x2, info, mod_l, ln_g.reshape(1, D_MODEL), ln_b.reshape(1, D_MODEL))


def _moe_sublayer(x2, h2, mod_l, w_router, w1, w3, w2, ln_g, ln_b, seq):
    T = x2.shape[0]
    tm = MOE_TM
    wr_pad = jnp.zeros((D_MODEL, LANE), BF16).at[:, :N_EXPERTS].set(w_router.astype(BF16))
    info, cnt = _router(h2, wr_pad)
    experts = info[:, INFO_I:INFO_I + TOP_K].astype(jnp.int32)
    ranks = info[:, INFO_R:INFO_R + TOP_K].astype(jnp.int32)
    counts = cnt[0, :N_EXPERTS].astype(jnp.int32)
    tiles_e = (counts + tm - 1) // tm
    tile_end = jnp.cumsum(tiles_e)
    tile_start = tile_end - tiles_e
    max_tiles = (T * TOP_K + N_EXPERTS * (tm - 1)) // tm + 2
    slots = (tile_start * tm)[experts] + ranks
    tile_ids = jnp.arange(max_tiles, dtype=jnp.int32)
    tile_expert = jnp.minimum(jnp.sum(tile_ids[:, None] >= tile_end[None, :], axis=1), N_EXPERTS - 1).astype(jnp.int32)
    n_tiles = tile_end[-1:].astype(jnp.int32)
    tile_rows = counts[tile_expert] - (tile_ids - tile_start[tile_expert]) * tm
    short = (tile_rows <= MOE_SHORT_ROWS).astype(jnp.int32)
    token_ids = jnp.broadcast_to(jnp.arange(T, dtype=jnp.int32)[:, None], (T, TOP_K))
    n_slots = max_tiles * tm
    spare = TOP_K * T + jnp.arange(n_slots, dtype=jnp.int32) % tm
    out_rows = jnp.arange(TOP_K, dtype=jnp.int32)[None, :] * T + token_ids
    slot_dest = spare.at[slots.reshape(-1)].set(out_rows.reshape(-1))
    slot_token = slot_dest % T
    dest = jnp.concatenate([spare[:tm], slot_dest[:n_slots - tm]])
    y_tok = _expert_ffn(h2, tile_expert, n_tiles, slot_token, dest, short, w1, w3, w2, TOP_K * T + tm)
    return _moe_combine(y_tok, x2, info, mod_l, ln_g, ln_b, seq)


def _pack_w_in(w_in_l):
    low = jnp.zeros((D_MODEL, N_PROJ - U_LOW * LANE), w_in_l.dtype).at[:, :GLA_RANK].set(w_in_l[:, _REF_LOW0:_REF_G0])
    a_cols = [w_in_l[:, part * A_WIDTH + g * W4:part * A_WIDTH + (g + 1) * W4]
              for g in range(len(DIL_GROUPS)) for part in range(3)]
    return jnp.concatenate(a_cols + [w_in_l[:, 3 * A_WIDTH:_REF_LOW0], w_in_l[:, _REF_G0:], low], axis=1).astype(BF16)


def _pack_w_up(w_up_l):
    pad = jnp.zeros((LANE, GLA_DK), F32).at[:GLA_RANK].set(w_up_l)
    hi = pad.astype(BF16)
    lo = (pad - hi.astype(F32)).astype(BF16)
    return hi, lo


def kernel(x, c, w_ada, b_ada, ln_g, ln_b, w_in, w_alpha_up, b_alpha, gla_norm_g, b_merge, w_branch_a,
           w_branch_b, w_out, ffn_w1, ffn_w3, ffn_w2, w_router, moe_w1, moe_w3, moe_w2):
    batch, seq, _ = x.shape
    T = batch * seq
    x2 = x.reshape(T, D_MODEL)
    mod = _adaln_mod(c, w_ada, b_ada)
    cast_block = {"w1": (1, D_MODEL, MXU_DIM), "w3": (1, D_MODEL, MXU_DIM), "w2": (1, MXU_DIM, D_MODEL)}
    to_cast = [(l, name, w[l // 2]) for l in range(DEPTH) if l % 2 == 1
               for name, w in (("w1", moe_w1), ("w2", moe_w2), ("w3", moe_w3))]
    moe_bf16 = {}

    def next_side(before_layer):
        if to_cast and to_cast[0][0] >= before_layer:
            layer, name, w = to_cast.pop(0)
            return (layer, name), (w, cast_block[name])
        return None, None

    for l in range(DEPTH):
        mod_l = mod[l]
        key, side = next_side(l)
        proj, cast = _in_projection(x2, mod_l, _pack_w_in(w_in[l]), seq, side)
        if key is not None:
            moe_bf16[key] = cast
        groups = [_dilated_group(proj, batch, seq, g) for g in range(len(DIL_GROUPS))]
        w_up_hi, w_up_lo = _pack_w_up(w_alpha_up[l])
        y_b = _gla_branch(proj, batch, seq, w_up_hi, w_up_lo, b_alpha[l], gla_norm_g[l])
        mixed = _branch_mix([g[0] for g in groups], [g[1] for g in groups], y_b, proj, b_merge[l],
                            w_branch_a[l].astype(BF16), w_branch_b[l].astype(BF16))
        moe_layer = l % 2 == 1
        x2, h2 = _out_projection(mixed, x2, mod_l, w_out[l].astype(BF16), ln_g[l, 0], ln_b[l, 0], seq, moe_layer)
        e = l // 2
        if moe_layer:
            to_cast = [item for item in to_cast if item[0] != l]
            w1, w3, w2 = (moe_bf16[(l, name)] if (l, name) in moe_bf16 else w[e].astype(BF16)
                          for name, w in (("w1", moe_w1), ("w3", moe_w3), ("w2", moe_w2)))
            x2 = _moe_sublayer(x2, h2, mod_l, w_router[e], w1, w3, w2, ln_g[l, 1], ln_b[l, 1], seq)
        else:
            key, side = next_side(l + 1)
            x2, cast = _dense_ffn(x2, mod_l, ffn_w1[e].astype(BF16), ffn_w3[e].astype(BF16), ffn_w2[e].astype(BF16),
                                  ln_g[l, 1], ln_b[l, 1], seq, side)
            if key is not None:
                moe_bf16[key] = cast
    return x2.reshape(batch, seq, D_MODEL)
```
